```python
import jax, jax.numpy as jnp
from jax import lax
import numpy as np

D_MODEL = 2048
BATCH = 16
SEQ = 2048
DEPTH = 1
DEC_BATCH = 4
DEC_SEQ = 4096
PAST_LEN = 128

N_HEADS = 8
QK_DIM = 64
V_DIM = 2 * QK_DIM
D_ATTN = N_HEADS * V_DIM
D_Q = N_HEADS * 2 * QK_DIM
D_K = N_HEADS * 2 * QK_DIM
Q_BLOCK = 128
POOL_WINDOWS = (2, 4, 8, 16)
N_POOL_GROUPS = len(POOL_WINDOWS)
D_POOL = D_MODEL // 2
POOL_GROUP_DIM = D_POOL // N_POOL_GROUPS
D_IN = D_Q + D_K + D_ATTN + D_POOL + 2 * D_MODEL
SPLITS = (D_Q, D_Q + D_K, D_Q + D_K + D_ATTN, D_Q + D_K + D_ATTN + D_POOL,
          D_Q + D_K + D_ATTN + D_POOL + D_MODEL)
D_FF = 5632
EPS = 1e-6

kernel_name = "hybrid_diffattn_pool_macaron_encoder"


def rmsnorm(x, g):
    xf = x.astype(jnp.float32)
    y = xf * lax.rsqrt(jnp.mean(xf * xf, axis=-1, keepdims=True) + EPS)
    return y.astype(x.dtype) * g


def swiglu(x, w_gu, w_down):
    gate, up = jnp.split(x @ w_gu, 2, axis=-1)
    return (jax.nn.silu(gate) * up) @ w_down


def alibi_slopes():
    return jnp.asarray(2.0 ** (-8.0 * np.arange(1, N_HEADS + 1) / N_HEADS), dtype=jnp.float32)


def diff_attention(q, k, v, lam, subln_g, lambda_init):
    B, S = q.shape[0], q.shape[1]
    n_blk = S // Q_BLOCK
    scale = QK_DIM ** -0.5
    slopes = alibi_slopes()
    key_pos = jnp.arange(S)
    qb = jnp.moveaxis(q.reshape(B, n_blk, Q_BLOCK, N_HEADS, 2, QK_DIM), 1, 0)

    def one_block(args):
        qi, start = args
        s = jnp.einsum('bqhcd,bkhcd->bhcqk', qi, k).astype(jnp.float32) * scale
        q_pos = start + jnp.arange(Q_BLOCK)
        dist = jnp.abs(q_pos[:, None] - key_pos[None, :]).astype(jnp.float32)
        s = s - slopes[None, :, None, None, None] * dist
        p = jax.nn.softmax(s, axis=-1)
        w = p[:, :, 0] - lam * p[:, :, 1]
        return jnp.einsum('bhqk,bkhd->bqhd', w.astype(v.dtype), v)

    out = lax.map(one_block, (qb, jnp.arange(n_blk) * Q_BLOCK))
    out = jnp.moveaxis(out, 0, 1).reshape(B, S, N_HEADS, V_DIM)
    out = rmsnorm(out, subln_g) * (1.0 - lambda_init)
    return out.reshape(B, S, D_ATTN)


def pool_mixer(p, w_grp, scale):
    B, S = p.shape[0], p.shape[1]
    pg = p.reshape(B, S, N_POOL_GROUPS, POOL_GROUP_DIM).astype(jnp.float32)
    c = jnp.concatenate([jnp.zeros((B, 1, N_POOL_GROUPS, POOL_GROUP_DIM), jnp.float32),
                         jnp.cumsum(pg, axis=1)], axis=1)
    t = jnp.arange(S)
    means = []
    for g, w in enumerate(POOL_WINDOWS):
        lo = jnp.clip(t - w // 2, 0, S - 1)
        hi = jnp.clip(t + w // 2 - 1, 0, S - 1)
        cg = c[:, :, g]
        window_sum = cg[:, hi + 1] - cg[:, lo]
        means.append(window_sum / (hi - lo + 1).astype(jnp.float32)[None, :, None])
    pooled = jnp.stack(means, axis=2) - pg
    y = jnp.einsum('bsgc,gcd->bsgd', pooled.astype(p.dtype), w_grp)
    return y.reshape(B, S, D_POOL) * scale


def encoder_layer(x, l, ffn1_norm, ffn1_w_gu, ffn1_w_down, mix_norm, w_in,
                  lambda_q1, lambda_k1, lambda_q2, lambda_k2, attn_subln_g, w_attn_proj,
                  w_pool_grp, pool_scale, w_pool_proj, w_out,
                  ffn2_norm, ffn2_w_gu, ffn2_w_down):
    B, S = x.shape[0], x.shape[1]
    h = x + 0.5 * swiglu(rmsnorm(x, ffn1_norm[l]), ffn1_w_gu[l], ffn1_w_down[l])
    u = rmsnorm(h, mix_norm[l])
    z = u @ w_in[l]
    q, k, v, p_in, g_a, g_p = jnp.split(z, SPLITS, axis=-1)
    q = q.reshape(B, S, N_HEADS, 2, QK_DIM)
    k = k.reshape(B, S, N_HEADS, 2, QK_DIM)
    v = v.reshape(B, S, N_HEADS, V_DIM)
    lambda_init = 0.8 - 0.6 * float(np.exp(-0.3 * l))
    lam = (jnp.exp(jnp.sum(lambda_q1[l].astype(jnp.float32) * lambda_k1[l].astype(jnp.float32)))
           - jnp.exp(jnp.sum(lambda_q2[l].astype(jnp.float32) * lambda_k2[l].astype(jnp.float32)))
           + lambda_init)
    a = diff_attention(q, k, v, lam, attn_subln_g[l], lambda_init) @ w_attn_proj[l]
    p = pool_mixer(p_in, w_pool_grp[l], pool_scale[l]) @ w_pool_proj[l]
    merged = jax.nn.sigmoid(g_a) * a + jax.nn.sigmoid(g_p) * p
    h = h + merged @ w_out[l]
    h = h + 0.5 * swiglu(rmsnorm(h, ffn2_norm[l]), ffn2_w_gu[l], ffn2_w_down[l])
    return h


def setup_inputs(seed: int = 0) -> dict:
    key = jax.random.key(seed)
    ks = jax.random.split(key, 24)
    f32 = jnp.float32

    def nrm(k, shape, fan_in):
        return jax.random.normal(k, shape, f32) * (fan_in ** -0.5)

    def gain(k, shape):
        return jnp.ones(shape, f32) + 0.01 * jax.random.normal(k, shape, f32)

    L = DEPTH
    return {
        "x_prompt": jax.random.normal(ks[0], (BATCH, SEQ, D_MODEL), f32),
        "x_sample": jax.random.normal(ks[1], (DEC_BATCH, DEC_SEQ, D_MODEL), f32),
        "ffn1_norm": gain(ks[2], (L, D_MODEL)),
        "ffn1_w_gu": nrm(ks[3], (L, D_MODEL, 2 * D_FF), D_MODEL),
        "ffn1_w_down": nrm(ks[4], (L, D_FF, D_MODEL), D_FF),
        "mix_norm": gain(ks[5], (L, D_MODEL)),
        "w_in": nrm(ks[6], (L, D_MODEL, D_IN), D_MODEL),
        "lambda_q1": 0.1 * jax.random.normal(ks[7], (L, QK_DIM), f32),
        "lambda_k1": 0.1 * jax.random.normal(ks[8], (L, QK_DIM), f32),
        "lambda_q2": 0.1 * jax.random.normal(ks[9], (L, QK_DIM), f32),
        "lambda_k2": 0.1 * jax.random.normal(ks[10], (L, QK_DIM), f32),
        "attn_subln_g": gain(ks[11], (L, V_DIM)),
        "w_attn_proj": nrm(ks[12], (L, D_ATTN, D_MODEL), D_ATTN),
        "w_pool_grp": nrm(ks[13], (L, N_POOL_GROUPS, POOL_GROUP_DIM, POOL_GROUP_DIM), POOL_GROUP_DIM),
        "pool_scale": gain(ks[14], (L, D_POOL)),
        "w_pool_proj": nrm(ks[15], (L, D_POOL, D_MODEL), D_POOL),
        "w_out": nrm(ks[16], (L, D_MODEL, D_MODEL), D_MODEL),
        "ffn2_norm": gain(ks[17], (L, D_MODEL)),
        "ffn2_w_gu": nrm(ks[18], (L, D_MODEL, 2 * D_FF), D_MODEL),
        "ffn2_w_down": nrm(ks[19], (L, D_FF, D_MODEL), D_FF),
        "final_norm": gain(ks[20], (D_MODEL,)),
    }


def reference(x_prompt, x_sample, ffn1_norm, ffn1_w_gu, ffn1_w_down, mix_norm, w_in,
              lambda_q1, lambda_k1, lambda_q2, lambda_k2, attn_subln_g, w_attn_proj,
              w_pool_grp, pool_scale, w_pool_proj, w_out,
              ffn2_norm, ffn2_w_gu, ffn2_w_down, final_norm):
    def trunk(x):
        h = x
        for l in range(DEPTH):
            h = encoder_layer(h, l, ffn1_norm, ffn1_w_gu, ffn1_w_down, mix_norm, w_in,
                              lambda_q1, lambda_k1, lambda_q2, lambda_k2, attn_subln_g,
                              w_attn_proj, w_pool_grp, pool_scale, w_pool_proj, w_out,
                              ffn2_norm, ffn2_w_gu, ffn2_w_down)
        return rmsnorm(h, final_norm)

    y_prompt = trunk(x_prompt)
    y_sample = trunk(x_sample)
    return (y_prompt, y_sample)
```

```python
import functools

import jax
import jax.numpy as jnp
import numpy as np
from jax import lax
from jax.experimental import pallas as pl
from jax.experimental.pallas import tpu as pltpu

D_MODEL = 2048
N_HEADS = 8
QK_DIM = 64
V_DIM = 2 * QK_DIM
HEAD_W = 2 * QK_DIM
D_ATTN = N_HEADS * V_DIM
D_Q = N_HEADS * HEAD_W
POOL_WINDOWS = (2, 4, 8, 16)
POOL_GROUP_DIM = 256
D_POOL = len(POOL_WINDOWS) * POOL_GROUP_DIM
POOL_HALO = 8
D_FF = 5632
EPS = 1e-6
LAMBDA_INIT = 0.8 - 0.6 * float(np.exp(-0.3 * 0))

V7X_VMEM_LIMIT_BYTES = 56 * 1024 * 1024

F32 = jnp.float32
BF16 = jnp.bfloat16


def _params(semantics):
    return pltpu.CompilerParams(dimension_semantics=semantics,
                                vmem_limit_bytes=V7X_VMEM_LIMIT_BYTES)


def _rms(xf, g):
    return xf * lax.rsqrt(jnp.mean(xf * xf, axis=-1, keepdims=True) + EPS) * g


def _ffn_kernel(x_ref, g_ref, wg_ref, wu_ref, wd_ref, g2_ref, *refs, nf, emit_u):
    if emit_u:
        o_ref, u_ref, xn_ref, acc_ref = refs
    else:
        o_ref, xn_ref, acc_ref = refs
    f = pl.program_id(1)

    @pl.when(f == 0)
    def _():
        xn_ref[...] = _rms(x_ref[...], g_ref[...]).astype(BF16)

    xn = xn_ref[...]
    gate = jnp.dot(xn, wg_ref[...], preferred_element_type=F32)
    up = jnp.dot(xn, wu_ref[...], preferred_element_type=F32)
    act = (gate * jax.nn.sigmoid(gate) * up).astype(BF16)
    part = jnp.dot(act, wd_ref[...], preferred_element_type=F32)

    @pl.when(f == 0)
    def _():
        acc_ref[...] = part

    @pl.when(f > 0)
    def _():
        acc_ref[...] += part

    @pl.when(f == nf - 1)
    def _():
        h = x_ref[...] + 0.5 * acc_ref[...]
        if emit_u:
            o_ref[...] = h
            u_ref[...] = _rms(h, g2_ref[...]).astype(BF16)
        else:
            o_ref[...] = _rms(h, g2_ref[...])


def _ffn(x, g, w_gu, w_down, g2, *, emit_u, tm=512, tf=512):
    T = x.shape[0]
    nf = D_FF // tf
    grid = (T // tm, nf)
    in_specs = [
        pl.BlockSpec((tm, D_MODEL), lambda i, f: (i, 0)),
        pl.BlockSpec((1, D_MODEL), lambda i, f: (0, 0)),
        pl.BlockSpec((D_MODEL, tf), lambda i, f: (0, f)),
        pl.BlockSpec((D_MODEL, tf), lambda i, f: (0, nf + f)),
        pl.BlockSpec((tf, D_MODEL), lambda i, f: (f, 0)),
        pl.BlockSpec((1, D_MODEL), lambda i, f: (0, 0)),
    ]
    row_spec = pl.BlockSpec((tm, D_MODEL), lambda i, f: (i, 0))
    if emit_u:
        out_shape = (jax.ShapeDtypeStruct((T, D_MODEL), F32), jax.ShapeDtypeStruct((T, D_MODEL), BF16))
        out_specs = (row_spec, row_spec)
    else:
        out_shape = jax.ShapeDtypeStruct((T, D_MODEL), F32)
        out_specs = row_spec
    return pl.pallas_call(
        functools.partial(_ffn_kernel, nf=nf, emit_u=emit_u),
        grid=grid, in_specs=in_specs, out_specs=out_specs, out_shape=out_shape,
        scratch_shapes=[pltpu.VMEM((tm, D_MODEL), BF16), pltpu.VMEM((tm, D_MODEL), F32)],
        compiler_params=_params(("parallel", "arbitrary")),
        name="ffn_u" if emit_u else "ffn_final",
    )(x, g, w_gu, w_gu, w_down, g2)


def _mm_nn_kernel(x_ref, w_ref, o_ref):
    o_ref[...] = jnp.dot(x_ref[...], w_ref[...], preferred_element_type=F32).astype(o_ref.dtype)


def _mm_nn(x, w, out_dtype, *, tm=1024, tn=1024, name):
    T, K = x.shape
    N = w.shape[1]
    return pl.pallas_call(
        _mm_nn_kernel,
        grid=(T // tm, N // tn),
        in_specs=[pl.BlockSpec((tm, K), lambda i, j: (i, 0)),
                  pl.BlockSpec((K, tn), lambda i, j: (0, j))],
        out_specs=pl.BlockSpec((tm, tn), lambda i, j: (i, j)),
        out_shape=jax.ShapeDtypeStruct((T, N), out_dtype),
        compiler_params=_params(("parallel", "arbitrary")),
        name=name,
    )(x, w)


def _mm_nt_kernel(wt_ref, x_ref, o_ref):
    o_ref[...] = lax.dot_general(wt_ref[...], x_ref[...], (((1,), (1,)), ((), ())),
                                 preferred_element_type=F32).astype(o_ref.dtype)


def _mm_nt(wt, x, out_dtype, *, tm=1024, name):
    N, K = wt.shape
    T = x.shape[0]
    return pl.pallas_call(
        _mm_nt_kernel,
        grid=(T // tm,),
        in_specs=[pl.BlockSpec((N, K), lambda i: (0, 0)),
                  pl.BlockSpec((tm, K), lambda i: (i, 0))],
        out_specs=pl.BlockSpec((N, tm), lambda i: (0, i)),
        out_shape=jax.ShapeDtypeStruct((N, T), out_dtype),
        compiler_params=_params(("parallel",)),
        name=name,
    )(wt, x)


def _attn_kernel(slope_ref, lq1_ref, lk1_ref, lq2_ref, lk2_ref, q_ref, k_ref, vt_ref, g_ref, o_ref,
                 *, seq, tq, tk):
    h = pl.program_id(1)
    q0 = pl.program_id(2) * tq
    slope = slope_ref[h]

    lane = lax.broadcasted_iota(jnp.int32, (tq, HEAD_W), 1)
    qs = q_ref[...] * (QK_DIM ** -0.5)
    zero = jnp.zeros_like(qs)
    qmaps = (jnp.where(lane < QK_DIM, qs, zero), jnp.where(lane >= QK_DIM, qs, zero))

    d0 = (lax.broadcasted_iota(jnp.int32, (tk, tq), 0)
          - lax.broadcasted_iota(jnp.int32, (tk, tq), 1)).astype(F32)

    def body(c, carry):
        k0 = pl.multiple_of(c * tk, tk)
        kc = k_ref[pl.ds(k0, tk), :]
        vc = vt_ref[:, pl.ds(k0, tk)]
        bias = -slope * jnp.abs(d0 + (k0 - q0).astype(F32))
        new = []
        for qm, (m, l, acc) in zip(qmaps, carry):
            s = lax.dot_general(kc, qm, (((1,), (1,)), ((), ())), preferred_element_type=F32) + bias
            mn = jnp.maximum(m, jnp.max(s, axis=0, keepdims=True))
            alpha = jnp.exp(m - mn)
            e = jnp.exp(s - mn)
            l = alpha * l + jnp.sum(e, axis=0, keepdims=True)
            acc = alpha * acc + jnp.dot(vc, e.astype(BF16), preferred_element_type=F32)
            new.append((mn, l, acc))
        return tuple(new)

    init = tuple((jnp.full((1, tq), -1e30, F32), jnp.zeros((1, tq), F32), jnp.zeros((V_DIM, tq), F32))
                 for _ in range(2))
    (_, l1, a1), (_, l2, a2) = lax.fori_loop(0, seq // tk, body, init)

    lam = (jnp.exp(jnp.sum(lq1_ref[...] * lk1_ref[...], axis=-1, keepdims=True))
           - jnp.exp(jnp.sum(lq2_ref[...] * lk2_ref[...], axis=-1, keepdims=True)) + LAMBDA_INIT)
    ot = a1 * (1.0 / l1) - lam * (a2 * (1.0 / l2))
    o = ot.T
    o_ref[...] = (_rms(o, g_ref[...]) * (1.0 - LAMBDA_INIT)).astype(o_ref.dtype)


def _attention(qk, vt, slopes, lq1, lk1, lq2, lk2, subln_g, *, batch, seq, tq=256, tk=512):
    nq = seq // tq
    vec = pl.BlockSpec((1, QK_DIM), lambda b, h, i: (0, 0))
    return pl.pallas_call(
        functools.partial(_attn_kernel, seq=seq, tq=tq, tk=tk),
        grid=(batch, N_HEADS, nq),
        in_specs=[
            pl.BlockSpec(memory_space=pltpu.SMEM),
            vec, vec, vec, vec,
            pl.BlockSpec((tq, HEAD_W), lambda b, h, i: (b * nq + i, h)),
            pl.BlockSpec((seq, HEAD_W), lambda b, h, i: (b, N_HEADS + h)),
            pl.BlockSpec((V_DIM, seq), lambda b, h, i: (h, b)),
            pl.BlockSpec((1, V_DIM), lambda b, h, i: (0, 0)),
        ],
        out_specs=pl.BlockSpec((tq, V_DIM), lambda b, h, i: (b * nq + i, h)),
        out_shape=jax.ShapeDtypeStruct((batch * seq, D_ATTN), BF16),
        compiler_params=_params(("parallel", "parallel", "arbitrary")),
        name="diff_attn",
    )(slopes, lq1, lk1, lq2, lk2, qk, qk, vt, subln_g)


def _pool_kernel(prev_ref, x_ref, next_ref, w_ref, s_ref, o_ref, *, seq, tm):
    i = pl.program_id(1)
    n = tm + 2 * POOL_HALO
    t = i * tm + lax.broadcasted_iota(jnp.int32, (tm, 1), 0)
    first = i == 0
    last = i == pl.num_programs(1) - 1
    for g, w in enumerate(POOL_WINDOWS):
        cols = slice(g * POOL_GROUP_DIM, (g + 1) * POOL_GROUP_DIM)
        x = x_ref[:, cols]
        prev = jnp.where(first, 0.0, prev_ref[:, cols])
        nxt = jnp.where(last, 0.0, next_ref[:, cols])
        a = jnp.concatenate([prev, x, nxt], axis=0)
        span = 1
        while span < w:
            a = a + pltpu.roll(a, n - span, axis=0)
            span *= 2
        off = POOL_HALO - w // 2
        if off:
            a = pltpu.roll(a, n - off, axis=0)
        wsum = a[:tm]
        lo = jnp.maximum(t - w // 2, 0)
        hi = jnp.minimum(t + w // 2 - 1, seq - 1)
        pooled = wsum / (hi - lo + 1).astype(F32) - x
        y = jnp.dot(pooled.astype(BF16), w_ref[g], preferred_element_type=F32)
        o_ref[:, cols] = (y * s_ref[:, cols]).astype(o_ref.dtype)


def _pool(rest, w_grp, scale, *, batch, seq, col_block, tm=512):
    nt = seq // tm
    hb = tm // POOL_HALO
    n_hblk = seq // POOL_HALO
    return pl.pallas_call(
        functools.partial(_pool_kernel, seq=seq, tm=tm),
        grid=(batch, nt),
        in_specs=[
            pl.BlockSpec((POOL_HALO, D_POOL),
                         lambda b, i: (b * n_hblk + jnp.maximum(i * hb - 1, 0), col_block)),
            pl.BlockSpec((tm, D_POOL), lambda b, i: (b * nt + i, col_block)),
            pl.BlockSpec((POOL_HALO, D_POOL),
                         lambda b, i: (b * n_hblk + jnp.minimum((i + 1) * hb, n_hblk - 1), col_block)),
            pl.BlockSpec((len(POOL_WINDOWS), POOL_GROUP_DIM, POOL_GROUP_DIM), lambda b, i: (0, 0, 0)),
            pl.BlockSpec((1, D_POOL), lambda b, i: (0, 0)),
        ],
        out_specs=pl.BlockSpec((tm, D_POOL), lambda b, i: (b * nt + i, 0)),
        out_shape=jax.ShapeDtypeStruct((batch * seq, D_POOL), BF16),
        compiler_params=_params(("parallel", "arbitrary")),
        name="pool_mixer",
    )(rest, rest, rest, w_grp, scale)


def _mix_kernel(at_ref, y_ref, ga_ref, gp_ref, h_ref, wa_ref, wp_ref, wo_ref, o_ref):
    a = jnp.dot(at_ref[...], wa_ref[...], preferred_element_type=F32)
    p = jnp.dot(y_ref[...], wp_ref[...], preferred_element_type=F32)
    merged = jax.nn.sigmoid(ga_ref[...]) * a + jax.nn.sigmoid(gp_ref[...]) * p
    o_ref[...] = h_ref[...] + jnp.dot(merged.astype(BF16), wo_ref[...], preferred_element_type=F32)


def _mix(attn, y, rest, h, w_a, w_p, w_o, *, tm=256):
    T = h.shape[0]
    const = lambda i: (0, 0)
    return pl.pallas_call(
        _mix_kernel,
        grid=(T // tm,),
        in_specs=[
            pl.BlockSpec((tm, D_ATTN), lambda i: (i, 0)),
            pl.BlockSpec((tm, D_POOL), lambda i: (i, 0)),
            pl.BlockSpec((tm, D_MODEL), lambda i: (i, 0)),
            pl.BlockSpec((tm, D_MODEL), lambda i: (i, 1)),
            pl.BlockSpec((tm, D_MODEL), lambda i: (i, 0)),
            pl.BlockSpec((D_ATTN, D_MODEL), const, pipeline_mode=pl.Buffered(1)),
            pl.BlockSpec((D_POOL, D_MODEL), const, pipeline_mode=pl.Buffered(1)),
            pl.BlockSpec((D_MODEL, D_MODEL), const, pipeline_mode=pl.Buffered(1)),
        ],
        out_specs=pl.BlockSpec((tm, D_MODEL), lambda i: (i, 0)),
        out_shape=jax.ShapeDtypeStruct((T, D_MODEL), F32),
        compiler_params=_params(("parallel",)),
        name="gated_mix",
    )(attn, y, rest, rest, h, w_a, w_p, w_o)


def _trunk(x, p):
    batch, seq, _ = x.shape
    xf = x.reshape(batch * seq, D_MODEL)
    h, u = _ffn(xf, p["ffn1_norm"], p["ffn1_w_gu"], p["ffn1_w_down"], p["mix_norm"], emit_u=True)
    qk = _mm_nn(u, p["w_qk"], BF16, name="proj_qk")
    vt = _mm_nt(p["w_v_t"], u, BF16, name="proj_vt")
    rest = _mm_nn(u, p["w_rest"], F32, name="proj_rest")
    attn = _attention(qk, vt, p["slopes"], p["lq1"], p["lk1"], p["lq2"], p["lk2"], p["subln_g"],
                      batch=batch, seq=seq)
    y = _pool(rest, p["w_pool_grp"], p["pool_scale"], batch=batch, seq=seq,
              col_block=2 * D_MODEL // D_POOL)
    h2 = _mix(attn, y, rest, h, p["w_attn_proj"], p["w_pool_proj"], p["w_out"])
    out = _ffn(h2, p["ffn2_norm"], p["ffn2_w_gu"], p["ffn2_w_down"], p["final_norm"], emit_u=False)
    return out.reshape(batch, seq, D_MODEL)


def kernel(x_prompt, x_sample, ffn1_norm, ffn1_w_gu, ffn1_w_down, mix_norm, w_in, lambda_q1, lambda_k1, lambda_q2, lambda_k2, attn_subln_g, w_attn_proj, w_pool_grp, pool_scale, w_pool_proj, w_out, ffn2_norm, ffn2_w_gu, ffn2_w_down, final_norm):
    l = 0
    w = w_in[l]
    c_v, c_p, c_ga = 2 * D_Q, 2 * D_Q + D_ATTN, 2 * D_Q + D_ATTN + D_POOL
    p = {
        "ffn1_norm": ffn1_norm[l][None], "mix_norm": mix_norm[l][None],
        "ffn2_norm": ffn2_norm[l][None], "final_norm": final_norm[None],
        "ffn1_w_gu": ffn1_w_gu[l].astype(BF16), "ffn1_w_down": ffn1_w_down[l].astype(BF16),
        "ffn2_w_gu": ffn2_w_gu[l].astype(BF16), "ffn2_w_down": ffn2_w_down[l].astype(BF16),
        "w_qk": w[:, :c_v].astype(BF16),
        "w_v_t": w[:, c_v:c_p].T.astype(BF16),
        "w_rest": jnp.concatenate([w[:, c_ga:], w[:, c_p:c_ga]], axis=1).astype(BF16),
        "slopes": jnp.asarray(2.0 ** (-8.0 * np.arange(1, N_HEADS + 1) / N_HEADS), dtype=F32),
        "lq1": lambda_q1[l][None], "lk1": lambda_k1[l][None],
        "lq2": lambda_q2[l][None], "lk2": lambda_k2[l][None],
        "subln_g": attn_subln_g[l][None],
        "w_attn_proj": w_attn_proj[l].astype(BF16),
        "w_pool_grp": w_pool_grp[l].astype(BF16),
        "pool_scale": pool_scale[l][None],
        "w_pool_proj": w_pool_proj[l].astype(BF16),
        "w_out": w_out[l].astype(BF16),
    }
    return (_trunk(x_prompt, p), _trunk(x_sample, p))
```

```python
import functools

import jax
import jax.numpy as jnp
import numpy as np
from jax import lax
from jax.experimental import pallas as pl
from jax.experimental.pallas import tpu as pltpu

D_MODEL = 2048
N_HEADS = 8
QK_DIM = 64
V_DIM = 2 * QK_DIM
HEAD_W = 2 * QK_DIM
D_ATTN = N_HEADS * V_DIM
D_Q = N_HEADS * HEAD_W
ALIBI_SPLIT = 64
N_BIAS_ROWS = 5
POOL_WINDOWS = (2, 4, 8, 16)
POOL_GROUP_DIM = 256
D_POOL = len(POOL_WINDOWS) * POOL_GROUP_DIM
POOL_HALO = 8
D_FF = 5632
EPS = 1e-6
LAMBDA_INIT = 0.8 - 0.6 * float(np.exp(-0.3 * 0))

V7X_VMEM_LIMIT_BYTES = 56 * 1024 * 1024

F32 = jnp.float32
BF16 = jnp.bfloat16


def _params(semantics):
    return pltpu.CompilerParams(dimension_semantics=semantics,
                                vmem_limit_bytes=V7X_VMEM_LIMIT_BYTES)


def _rms(xf, g):
    return xf * lax.rsqrt(jnp.mean(xf * xf, axis=-1, keepdims=True) + EPS) * g


def _ffn_kernel(x_ref, g_ref, wg_ref, wu_ref, wd_ref, g2_ref, *refs, nf, emit_u):
    if emit_u:
        o_ref, u_ref, xn_ref, acc_ref = refs
    else:
        o_ref, xn_ref, acc_ref = refs
    f = pl.program_id(1)

    @pl.when(f == 0)
    def _():
        xn_ref[...] = _rms(x_ref[...], g_ref[...]).astype(BF16)
        acc_ref[...] = jnp.zeros_like(acc_ref)

    xn = xn_ref[...]
    gate = jnp.dot(xn, wg_ref[...], preferred_element_type=F32)
    up = jnp.dot(xn, wu_ref[...], preferred_element_type=F32)
    act = (gate * jax.nn.sigmoid(gate) * up).astype(BF16)
    acc_ref[...] += jnp.dot(act, wd_ref[...], preferred_element_type=F32)

    @pl.when(f == nf - 1)
    def _():
        h = x_ref[...] + 0.5 * acc_ref[...]
        if emit_u:
            o_ref[...] = h
            u_ref[...] = _rms(h, g2_ref[...]).astype(BF16)
        else:
            o_ref[...] = _rms(h, g2_ref[...])


def _ffn(x, g, w_gu, w_down, g2, *, emit_u, tm=512, tf=512):
    T = x.shape[0]
    nf = D_FF // tf
    grid = (T // tm, nf)
    in_specs = [
        pl.BlockSpec((tm, D_MODEL), lambda i, f: (i, 0)),
        pl.BlockSpec((1, D_MODEL), lambda i, f: (0, 0)),
        pl.BlockSpec((D_MODEL, tf), lambda i, f: (0, f)),
        pl.BlockSpec((D_MODEL, tf), lambda i, f: (0, nf + f)),
        pl.BlockSpec((tf, D_MODEL), lambda i, f: (f, 0)),
        pl.BlockSpec((1, D_MODEL), lambda i, f: (0, 0)),
    ]
    row_spec = pl.BlockSpec((tm, D_MODEL), lambda i, f: (i, 0))
    if emit_u:
        out_shape = (jax.ShapeDtypeStruct((T, D_MODEL), F32), jax.ShapeDtypeStruct((T, D_MODEL), BF16))
        out_specs = (row_spec, row_spec)
    else:
        out_shape = jax.ShapeDtypeStruct((T, D_MODEL), F32)
        out_specs = row_spec
    return pl.pallas_call(
        functools.partial(_ffn_kernel, nf=nf, emit_u=emit_u),
        grid=grid, in_specs=in_specs, out_specs=out_specs, out_shape=out_shape,
        scratch_shapes=[pltpu.VMEM((tm, D_MODEL), BF16), pltpu.VMEM((tm, D_MODEL), F32)],
        compiler_params=_params(("parallel", "arbitrary")),
        name="ffn_u" if emit_u else "ffn_final",
    )(x, g, w_gu, w_gu, w_down, g2)


def _mm_nn_kernel(x_ref, w_ref, o_ref):
    o_ref[...] = jnp.dot(x_ref[...], w_ref[...], preferred_element_type=F32).astype(o_ref.dtype)


def _mm_nn(x, w, out_dtype, *, tm=1024, tn=1024, name):
    T, K = x.shape
    N = w.shape[1]
    return pl.pallas_call(
        _mm_nn_kernel,
        grid=(T // tm, N // tn),
        in_specs=[pl.BlockSpec((tm, K), lambda i, j: (i, 0)),
                  pl.BlockSpec((K, tn), lambda i, j: (0, j))],
        out_specs=pl.BlockSpec((tm, tn), lambda i, j: (i, j)),
        out_shape=jax.ShapeDtypeStruct((T, N), out_dtype),
        compiler_params=_params(("parallel", "arbitrary")),
        name=name,
    )(x, w)


def _mm_nt_kernel(wt_ref, x_ref, o_ref):
    o_ref[...] = lax.dot_general(wt_ref[...], x_ref[...], (((1,), (1,)), ((), ())),
                                 preferred_element_type=F32).astype(o_ref.dtype)


def _mm_nt(wt, x, out_dtype, *, tm=1024, name):
    N, K = wt.shape
    T = x.shape[0]
    return pl.pallas_call(
        _mm_nt_kernel,
        grid=(T // tm,),
        in_specs=[pl.BlockSpec((N, K), lambda i: (0, 0)),
                  pl.BlockSpec((tm, K), lambda i: (i, 0))],
        out_specs=pl.BlockSpec((N, tm), lambda i: (0, i)),
        out_shape=jax.ShapeDtypeStruct((N, T), out_dtype),
        compiler_params=_params(("parallel",)),
        name=name,
    )(wt, x)


def _attn_kernel(slope_ref, lq1_ref, lk1_ref, lq2_ref, lk2_ref, q_ref, k_ref, vt_ref, g_ref, o_ref,
                 ka_ref, corr_ref, s_ref, *, seq, t):
    h = pl.program_id(1)
    i = pl.program_id(2)
    n = seq // t
    c = slope_ref[h]

    @pl.when(i == 0)
    def _():
        j = lax.broadcasted_iota(jnp.int32, (seq, HEAD_W), 0)
        lane = lax.broadcasted_iota(jnp.int32, (seq, HEAD_W), 1)
        j_lo = j & (ALIBI_SPLIT - 1)
        pos = jnp.where(lane == 0, j_lo, jnp.where(lane == 1, j - j_lo, jnp.where(lane < N_BIAS_ROWS, 1, 0)))
        ka_ref[:, :HEAD_W] = k_ref[...]
        ka_ref[:, HEAD_W:] = pos.astype(F32).astype(BF16)
        d = lax.broadcasted_iota(jnp.int32, (t, t), 0) - lax.broadcasted_iota(jnp.int32, (t, t), 1)
        corr = (-2.0 * c) * jnp.maximum(d, 0).astype(F32)
        corr_ref[:, :t] = corr
        corr_ref[:, t:] = corr

    qt = q_ref[...].astype(F32).T * (QK_DIM ** -0.5)
    row = lax.broadcasted_iota(jnp.int32, (HEAD_W, t), 0)
    col = lax.broadcasted_iota(jnp.int32, (HEAD_W, t), 1)
    col_lo = col & (ALIBI_SPLIT - 1)
    q0c = c * (i * t).astype(F32)
    bias_l = jnp.where(row < 2, c, jnp.where(row == 2, -q0c, 0.0))
    bias_r = jnp.where(row < 2, -c, jnp.where(row == 2, q0c, jnp.where(
        row == 3, (2.0 * c) * col_lo.astype(F32), jnp.where(row == 4, (2.0 * c) * (col - col_lo).astype(F32), 0.0))))
    main = jnp.concatenate([jnp.where(row < QK_DIM, qt, 0.0), jnp.where(row >= QK_DIM, qt, 0.0)], axis=1)
    q_l = jnp.concatenate([main, jnp.concatenate([bias_l, bias_l], axis=1)], axis=0).astype(BF16)
    q_r = jnp.concatenate([main, jnp.concatenate([bias_r, bias_r], axis=1)], axis=0).astype(BF16)

    def score_step(ci, m, q_aug, on_diagonal=False):
        r0 = pl.multiple_of(ci * t, t)
        s = jnp.dot(ka_ref[pl.ds(r0, t), :], q_aug, preferred_element_type=F32)
        if on_diagonal:
            s = s + corr_ref[...]
        s_ref[pl.ds(r0, t), :] = s
        return jnp.maximum(m, jnp.max(s, axis=0, keepdims=True))

    m = jnp.full((1, 2 * t), -1e30, F32)
    m = lax.fori_loop(0, i, functools.partial(score_step, q_aug=q_l), m)
    m = score_step(i, m, q_l, on_diagonal=True)
    m = lax.fori_loop(i + 1, n, functools.partial(score_step, q_aug=q_r), m)

    def value_step(ci, carry):
        l, acc = carry
        r0 = pl.multiple_of(ci * t, t)
        e = jnp.exp(s_ref[pl.ds(r0, t), :] - m)
        l = l + jnp.sum(e, axis=0, keepdims=True)
        acc = acc + jnp.dot(vt_ref[:, pl.ds(r0, t)], e.astype(BF16), preferred_element_type=F32)
        return l, acc

    l, acc = lax.fori_loop(0, n, value_step, (jnp.zeros((1, 2 * t), F32), jnp.zeros((V_DIM, 2 * t), F32)))
    out = acc * (1.0 / l)

    lam = (jnp.exp(jnp.sum(lq1_ref[...] * lk1_ref[...], axis=-1, keepdims=True))
           - jnp.exp(jnp.sum(lq2_ref[...] * lk2_ref[...], axis=-1, keepdims=True)) + LAMBDA_INIT)
    o = (out[:, :t] - lam * out[:, t:]).T
    o_ref[...] = (_rms(o, g_ref[...]) * (1.0 - LAMBDA_INIT)).astype(o_ref.dtype)


def _attention(qk, vt, slopes, lq1, lk1, lq2, lk2, subln_g, *, batch, seq, t=512):
    nq = seq // t
    vec = pl.BlockSpec((1, QK_DIM), lambda b, h, i: (0, 0))
    return pl.pallas_call(
        functools.partial(_attn_kernel, seq=seq, t=t),
        grid=(batch, N_HEADS, nq),
        in_specs=[
            pl.BlockSpec(memory_space=pltpu.SMEM),
            vec, vec, vec, vec,
            pl.BlockSpec((t, HEAD_W), lambda b, h, i: (b * nq + i, h)),
            pl.BlockSpec((seq, HEAD_W), lambda b, h, i: (b, N_HEADS + h)),
            pl.BlockSpec((V_DIM, seq), lambda b, h, i: (h, b)),
            pl.BlockSpec((1, V_DIM), lambda b, h, i: (0, 0)),
        ],
        out_specs=pl.BlockSpec((t, V_DIM), lambda b, h, i: (b * nq + i, h)),
        out_shape=jax.ShapeDtypeStruct((batch * seq, D_ATTN), BF16),
        scratch_shapes=[
            pltpu.VMEM((seq, 2 * HEAD_W), BF16),
            pltpu.VMEM((t, 2 * t), F32),
            pltpu.VMEM((seq, 2 * t), F32),
        ],
        compiler_params=_params(("parallel", "parallel", "arbitrary")),
        name="diff_attn",
    )(slopes, lq1, lk1, lq2, lk2, qk, qk, vt, subln_g)


def _pool_kernel(prev_ref, x_ref, next_ref, w_ref, s_ref, o_ref, *, seq, tm):
    i = pl.program_id(1)
    n = tm + 2 * POOL_HALO
    t = i * tm + lax.broadcasted_iota(jnp.int32, (tm, 1), 0)
    first = i == 0
    last = i == pl.num_programs(1) - 1
    for g, w in enumerate(POOL_WINDOWS):
        cols = slice(g * POOL_GROUP_DIM, (g + 1) * POOL_GROUP_DIM)
        x = x_ref[:, cols]
        prev = jnp.where(first, 0.0, prev_ref[:, cols])
        nxt = jnp.where(last, 0.0, next_ref[:, cols])
        a = jnp.concatenate([prev, x, nxt], axis=0)
        span = 1
        while span < w:
            a = a + pltpu.roll(a, n - span, axis=0)
            span *= 2
        off = POOL_HALO - w // 2
        if off:
            a = pltpu.roll(a, n - off, axis=0)
        wsum = a[:tm]
        lo = jnp.maximum(t - w // 2, 0)
        hi = jnp.minimum(t + w // 2 - 1, seq - 1)
        pooled = wsum / (hi - lo + 1).astype(F32) - x
        y = jnp.dot(pooled.astype(BF16), w_ref[g], preferred_element_type=F32)
        o_ref[:, cols] = (y * s_ref[:, cols]).astype(o_ref.dtype)


def _pool(rest, w_grp, scale, *, batch, seq, col_block, tm=512):
    nt = seq // tm
    hb = tm // POOL_HALO
    n_hblk = seq // POOL_HALO
    return pl.pallas_call(
        functools.partial(_pool_kernel, seq=seq, tm=tm),
        grid=(batch, nt),
        in_specs=[
            pl.BlockSpec((POOL_HALO, D_POOL),
                         lambda b, i: (b * n_hblk + jnp.maximum(i * hb - 1, 0), col_block)),
            pl.BlockSpec((tm, D_POOL), lambda b, i: (b * nt + i, col_block)),
            pl.BlockSpec((POOL_HALO, D_POOL),
                         lambda b, i: (b * n_hblk + jnp.minimum((i + 1) * hb, n_hblk - 1), col_block)),
            pl.BlockSpec((len(POOL_WINDOWS), POOL_GROUP_DIM, POOL_GROUP_DIM), lambda b, i: (0, 0, 0)),
            pl.BlockSpec((1, D_POOL), lambda b, i: (0, 0)),
        ],
        out_specs=pl.BlockSpec((tm, D_POOL), lambda b, i: (b * nt + i, 0)),
        out_shape=jax.ShapeDtypeStruct((batch * seq, D_POOL), BF16),
        compiler_params=_params(("parallel", "arbitrary")),
        name="pool_mixer",
    )(rest, rest, rest, w_grp, scale)


def _mix_kernel(at_ref, y_ref, ga_ref, gp_ref, h_ref, wa_ref, wp_ref, wo_ref, o_ref):
    a = jnp.dot(at_ref[...], wa_ref[...], preferred_element_type=F32)
    p = jnp.dot(y_ref[...], wp_ref[...], preferred_element_type=F32)
    merged = jax.nn.sigmoid(ga_ref[...]) * a + jax.nn.sigmoid(gp_ref[...]) * p
    o_ref[...] = h_ref[...] + jnp.dot(merged.astype(BF16), wo_ref[...], preferred_element_type=F32)


def _mix(attn, y, rest, h, w_a, w_p, w_o, *, tm=256):
    T = h.shape[0]
    const = lambda i: (0, 0)
    return pl.pallas_call(
        _mix_kernel,
        grid=(T // tm,),
        in_specs=[
            pl.BlockSpec((tm, D_ATTN), lambda i: (i, 0)),
            pl.BlockSpec((tm, D_POOL), lambda i: (i, 0)),
            pl.BlockSpec((tm, D_MODEL), lambda i: (i, 0)),
            pl.BlockSpec((tm, D_MODEL), lambda i: (i, 1)),
            pl.BlockSpec((tm, D_MODEL), lambda i: (i, 0)),
            pl.BlockSpec((D_ATTN, D_MODEL), const, pipeline_mode=pl.Buffered(1)),
            pl.BlockSpec((D_POOL, D_MODEL), const, pipeline_mode=pl.Buffered(1)),
            pl.BlockSpec((D_MODEL, D_MODEL), const, pipeline_mode=pl.Buffered(1)),
        ],
        out_specs=pl.BlockSpec((tm, D_MODEL), lambda i: (i, 0)),
        out_shape=jax.ShapeDtypeStruct((T, D_MODEL), F32),
        compiler_params=_params(("parallel",)),
        name="gated_mix",
    )(attn, y, rest, rest, h, w_a, w_p, w_o)


def _trunk(x, p):
    batch, seq, _ = x.shape
    xf = x.reshape(batch * seq, D_MODEL)
    h, u = _ffn(xf, p["ffn1_norm"], p["ffn1_w_gu"], p["ffn1_w_down"], p["mix_norm"], emit_u=True)
    qk = _mm_nn(u, p["w_qk"], BF16, name="proj_qk")
    vt = _mm_nt(p["w_v_t"], u, BF16, name="proj_vt")
    rest = _mm_nn(u, p["w_rest"], F32, name="proj_rest")
    attn = _attention(qk, vt, p["slopes"], p["lq1"], p["lk1"], p["lq2"], p["lk2"], p["subln_g"],
                      batch=batch, seq=seq)
    y = _pool(rest, p["w_pool_grp"], p["pool_scale"], batch=batch, seq=seq,
              col_block=2 * D_MODEL // D_POOL)
    h2 = _mix(attn, y, rest, h, p["w_attn_proj"], p["w_pool_proj"], p["w_out"])
    out = _ffn(h2, p["ffn2_norm"], p["ffn2_w_gu"], p["ffn2_w_down"], p["final_norm"], emit_u=False)
    return out.reshape(batch, seq, D_MODEL)


def kernel(x_prompt, x_sample, ffn1_norm, ffn1_w_gu, ffn1_w_down, mix_norm, w_in, lambda_q1, lambda_k1, lambda_q2, lambda_k2, attn_subln_g, w_attn_proj, w_pool_grp, pool_scale, w_pool_proj, w_out, ffn2_norm, ffn2_w_gu, ffn2_w_down, final_norm):
    l = 0
    w = w_in[l]
    c_v, c_p, c_ga = 2 * D_Q, 2 * D_Q + D_ATTN, 2 * D_Q + D_ATTN + D_POOL
    p = {
        "ffn1_norm": ffn1_norm[l][None], "mix_norm": mix_norm[l][None],
        "ffn2_norm": ffn2_norm[l][None], "final_norm": final_norm[None],
        "ffn1_w_gu": ffn1_w_gu[l].astype(BF16), "ffn1_w_down": ffn1_w_down[l].astype(BF16),
        "ffn2_w_gu": ffn2_w_gu[l].astype(BF16), "ffn2_w_down": ffn2_w_down[l].astype(BF16),
        "w_qk": w[:, :c_v].astype(BF16),
        "w_v_t": w[:, c_v:c_p].T.astype(BF16),
        "w_rest": jnp.concatenate([w[:, c_ga:], w[:, c_p:c_ga]], axis=1).astype(BF16),
        "slopes": jnp.asarray(2.0 ** (-8.0 * np.arange(1, N_HEADS + 1) / N_HEADS), dtype=F32),
        "lq1": lambda_q1[l][None], "lk1": lambda_k1[l][None],
        "lq2": lambda_q2[l][None], "lk2": lambda_k2[l][None],
        "subln_g": attn_subln_g[l][None],
        "w_attn_proj": w_attn_proj[l].astype(BF16),
        "w_pool_grp": w_pool_grp[l].astype(BF16),
        "pool_scale": pool_scale[l][None],
        "w_pool_proj": w_pool_proj[l].astype(BF16),
        "w_out": w_out[l].astype(BF16),
    }
    return (_trunk(x_prompt, p), _trunk(x_sample, p))
```

```python
import functools

import jax
import jax.numpy as jnp
import numpy as np
from jax import lax
from jax.experimental import pallas as pl
from jax.experimental.pallas import tpu as pltpu

D_MODEL = 2048
N_HEADS = 8
QK_DIM = 64
V_DIM = 2 * QK_DIM
HEAD_W = 2 * QK_DIM
D_ATTN = N_HEADS * V_DIM
D_Q = N_HEADS * HEAD_W
ALIBI_SPLIT = 64
BIAS_GROUP = 8
ONES_ROWS = 16
LOG2E = 1.4426950408889634
POOL_WINDOWS = (2, 4, 8, 16)
POOL_GROUP_DIM = 256
D_POOL = len(POOL_WINDOWS) * POOL_GROUP_DIM
POOL_HALO = 8
D_FF = 5632
EPS = 1e-6
LAMBDA_INIT = 0.8 - 0.6 * float(np.exp(-0.3 * 0))

V7X_VMEM_LIMIT_BYTES = 56 * 1024 * 1024

F32 = jnp.float32
BF16 = jnp.bfloat16


def _params(semantics):
    return pltpu.CompilerParams(dimension_semantics=semantics,
                                vmem_limit_bytes=V7X_VMEM_LIMIT_BYTES)


def _rms(xf, g):
    return xf * lax.rsqrt(jnp.mean(xf * xf, axis=-1, keepdims=True) + EPS) * g


def _ffn_kernel(x_ref, g_ref, wg_ref, wu_ref, wd_ref, g2_ref, *refs, nf, emit_u):
    if emit_u:
        o_ref, u_ref, xn_ref, acc_ref = refs
    else:
        o_ref, xn_ref, acc_ref = refs
    f = pl.program_id(1)

    @pl.when(f == 0)
    def _():
        xn_ref[...] = _rms(x_ref[...], g_ref[...]).astype(BF16)
        acc_ref[...] = jnp.zeros_like(acc_ref)

    xn = xn_ref[...]
    gate = jnp.dot(xn, wg_ref[...], preferred_element_type=F32)
    up = jnp.dot(xn, wu_ref[...], preferred_element_type=F32)
    act = (gate * jax.nn.sigmoid(gate) * up).astype(BF16)
    acc_ref[...] += jnp.dot(act, wd_ref[...], preferred_element_type=F32)

    @pl.when(f == nf - 1)
    def _():
        h = x_ref[...] + 0.5 * acc_ref[...]
        if emit_u:
            o_ref[...] = h
            u_ref[...] = _rms(h, g2_ref[...]).astype(BF16)
        else:
            o_ref[...] = _rms(h, g2_ref[...])


def _ffn(x, g, w_gu, w_down, g2, *, emit_u, tm=512, tf=512):
    T = x.shape[0]
    nf = D_FF // tf
    grid = (T // tm, nf)
    in_specs = [
        pl.BlockSpec((tm, D_MODEL), lambda i, f: (i, 0)),
        pl.BlockSpec((1, D_MODEL), lambda i, f: (0, 0)),
        pl.BlockSpec((D_MODEL, tf), lambda i, f: (0, f)),
        pl.BlockSpec((D_MODEL, tf), lambda i, f: (0, nf + f)),
        pl.BlockSpec((tf, D_MODEL), lambda i, f: (f, 0)),
        pl.BlockSpec((1, D_MODEL), lambda i, f: (0, 0)),
    ]
    row_spec = pl.BlockSpec((tm, D_MODEL), lambda i, f: (i, 0))
    if emit_u:
        out_shape = (jax.ShapeDtypeStruct((T, D_MODEL), F32), jax.ShapeDtypeStruct((T, D_MODEL), BF16))
        out_specs = (row_spec, row_spec)
    else:
        out_shape = jax.ShapeDtypeStruct((T, D_MODEL), F32)
        out_specs = row_spec
    return pl.pallas_call(
        functools.partial(_ffn_kernel, nf=nf, emit_u=emit_u),
        grid=grid, in_specs=in_specs, out_specs=out_specs, out_shape=out_shape,
        scratch_shapes=[pltpu.VMEM((tm, D_MODEL), BF16), pltpu.VMEM((tm, D_MODEL), F32)],
        compiler_params=_params(("parallel", "arbitrary")),
        name="ffn_u" if emit_u else "ffn_final",
    )(x, g, w_gu, w_gu, w_down, g2)


def _mm_nn_kernel(x_ref, w_ref, o_ref):
    o_ref[...] = jnp.dot(x_ref[...], w_ref[...], preferred_element_type=F32).astype(o_ref.dtype)


def _mm_nn(x, w, out_dtype, *, tm=1024, tn=1024, name):
    T, K = x.shape
    N = w.shape[1]
    return pl.pallas_call(
        _mm_nn_kernel,
        grid=(T // tm, N // tn),
        in_specs=[pl.BlockSpec((tm, K), lambda i, j: (i, 0)),
                  pl.BlockSpec((K, tn), lambda i, j: (0, j))],
        out_specs=pl.BlockSpec((tm, tn), lambda i, j: (i, j)),
        out_shape=jax.ShapeDtypeStruct((T, N), out_dtype),
        compiler_params=_params(("parallel", "arbitrary")),
        name=name,
    )(x, w)


def _mm_nt_kernel(wt_ref, x_ref, o_ref):
    o_ref[...] = lax.dot_general(wt_ref[...], x_ref[...], (((1,), (1,)), ((), ())),
                                 preferred_element_type=F32).astype(o_ref.dtype)


def _mm_nt(wt, x, out_dtype, *, tm=1024, name):
    N, K = wt.shape
    T = x.shape[0]
    return pl.pallas_call(
        _mm_nt_kernel,
        grid=(T // tm,),
        in_specs=[pl.BlockSpec((N, K), lambda i: (0, 0)),
                  pl.BlockSpec((tm, K), lambda i: (i, 0))],
        out_specs=pl.BlockSpec((N, tm), lambda i: (0, i)),
        out_shape=jax.ShapeDtypeStruct((N, T), out_dtype),
        compiler_params=_params(("parallel",)),
        name=name,
    )(wt, x)


def _attn_kernel(slope_ref, lq1_ref, lk1_ref, lq2_ref, lk2_ref, q_ref, k_ref, vt_ref, g_ref, o_ref,
                 ka_ref, vta_ref, corr_ref, s_ref, *, seq, t):
    h = pl.program_id(1)
    i = pl.program_id(2)
    n = seq // t
    c2 = slope_ref[h] * LOG2E

    @pl.when(i == 0)
    def _():
        j = lax.broadcasted_iota(jnp.int32, (seq, HEAD_W), 0)
        lane = lax.broadcasted_iota(jnp.int32, (seq, HEAD_W), 1)
        sub = lane & (BIAS_GROUP - 1)
        j_lo = j & (ALIBI_SPLIT - 1)
        pos = jnp.where(lane < 3 * BIAS_GROUP,
                        jnp.where(sub == 0, j_lo, jnp.where(sub == 1, j - j_lo, jnp.where(sub == 2, 1, 0))), 0)
        pos = pos.astype(F32).astype(BF16)
        kf = k_ref[...].astype(F32)
        for mp in range(2):
            km = kf[:, mp * QK_DIM:(mp + 1) * QK_DIM]
            ka_ref[mp, :, :HEAD_W] = jnp.concatenate([km, km], axis=1).astype(BF16)
            ka_ref[mp, :, HEAD_W:] = pos
        vta_ref[:V_DIM, :] = vt_ref[...]
        vta_ref[V_DIM:, :] = jnp.ones((ONES_ROWS, seq), BF16)
        d = lax.broadcasted_iota(jnp.int32, (t, t), 0) - lax.broadcasted_iota(jnp.int32, (t, t), 1)
        corr_ref[:t, :] = (-2.0 * c2) * jnp.maximum(d, 0).astype(F32)
        corr_ref[t:, :] = jnp.zeros((t, t), F32)

    qa = q_ref[...].astype(F32).T * (QK_DIM ** -0.5 * LOG2E)
    q_hi = qa.astype(BF16)
    q_lo = (qa - q_hi.astype(F32)).astype(BF16)
    mains = [jnp.concatenate([q_hi[mp * QK_DIM:(mp + 1) * QK_DIM], q_lo[mp * QK_DIM:(mp + 1) * QK_DIM]], axis=0)
             for mp in range(2)]

    row = lax.broadcasted_iota(jnp.int32, (4 * BIAS_GROUP, t), 0)
    col = lax.broadcasted_iota(jnp.int32, (4 * BIAS_GROUP, t), 1)
    sub = row & (BIAS_GROUP - 1)
    q0c = c2 * (i * t).astype(F32)

    def bias_rows(base):
        hi = base.astype(BF16).astype(F32)
        mid = (base - hi).astype(BF16).astype(F32)
        lo = base - hi - mid
        b = jnp.where(row < BIAS_GROUP, hi, jnp.where(row < 2 * BIAS_GROUP, mid,
                                                      jnp.where(row < 3 * BIAS_GROUP, lo, 0.0)))
        return jnp.concatenate([b, jnp.zeros((HEAD_W - 4 * BIAS_GROUP, t), F32)], axis=0).astype(BF16)

    bias_l = bias_rows(jnp.where(sub < 2, c2, jnp.where(sub == 2, -q0c, 0.0)))
    bias_r = bias_rows(jnp.where(sub < 2, -c2, jnp.where(sub == 2, q0c + (2.0 * c2) * col.astype(F32), 0.0)))

    def score_step(ci, ms):
        r0 = pl.multiple_of(ci * t, t)
        bias = jnp.where(ci <= i, bias_l, bias_r)
        corr = corr_ref[pl.ds(pl.multiple_of(jnp.where(ci == i, 0, t), t), t), :]
        new = []
        for mp in range(2):
            q_aug = jnp.concatenate([mains[mp], bias], axis=0)
            s = jnp.dot(ka_ref[mp, pl.ds(r0, t), :], q_aug, preferred_element_type=F32) + corr
            s_ref[pl.ds(r0, t), mp * t:(mp + 1) * t] = s
            new.append(jnp.maximum(ms[mp], jnp.max(s, axis=0, keepdims=True)))
        return tuple(new)

    m_init = jnp.full((1, t), -1e30, F32)
    ms = lax.fori_loop(0, n, score_step, (m_init, m_init), unroll=True)
    m = jnp.concatenate(ms, axis=1)

    def value_step(ci, acc):
        r0 = pl.multiple_of(ci * t, t)
        e = jnp.exp2(s_ref[pl.ds(r0, t), :] - m).astype(BF16)
        return acc + jnp.dot(vta_ref[:, pl.ds(r0, t)], e, preferred_element_type=F32)

    acc = lax.fori_loop(0, n, value_step, jnp.zeros((V_DIM + ONES_ROWS, 2 * t), F32), unroll=4)
    out = acc[:V_DIM] * (1.0 / acc[V_DIM:V_DIM + 1])

    lam = (jnp.exp(jnp.sum(lq1_ref[...] * lk1_ref[...], axis=-1, keepdims=True))
           - jnp.exp(jnp.sum(lq2_ref[...] * lk2_ref[...], axis=-1, keepdims=True)) + LAMBDA_INIT)
    o = (out[:, :t] - lam * out[:, t:]).T
    o_ref[...] = (_rms(o, g_ref[...]) * (1.0 - LAMBDA_INIT)).astype(o_ref.dtype)


def _attention(qk, vt, slopes, lq1, lk1, lq2, lk2, subln_g, *, batch, seq, t=512):
    nq = seq // t
    vec = pl.BlockSpec((1, QK_DIM), lambda b, h, i: (0, 0))
    return pl.pallas_call(
        functools.partial(_attn_kernel, seq=seq, t=t),
        grid=(batch, N_HEADS, nq),
        in_specs=[
            pl.BlockSpec(memory_space=pltpu.SMEM),
            vec, vec, vec, vec,
            pl.BlockSpec((t, HEAD_W), lambda b, h, i: (b * nq + i, h)),
            pl.BlockSpec((seq, HEAD_W), lambda b, h, i: (b, N_HEADS + h)),
            pl.BlockSpec((V_DIM, seq), lambda b, h, i: (h, b)),
            pl.BlockSpec((1, V_DIM), lambda b, h, i: (0, 0)),
        ],
        out_specs=pl.BlockSpec((t, V_DIM), lambda b, h, i: (b * nq + i, h)),
        out_shape=jax.ShapeDtypeStruct((batch * seq, D_ATTN), BF16),
        scratch_shapes=[
            pltpu.VMEM((2, seq, 2 * HEAD_W), BF16),
            pltpu.VMEM((V_DIM + ONES_ROWS, seq), BF16),
            pltpu.VMEM((2 * t, t), F32),
            pltpu.VMEM((seq, 2 * t), F32),
        ],
        compiler_params=_params(("parallel", "parallel", "arbitrary")),
        name="diff_attn",
    )(slopes, lq1, lk1, lq2, lk2, qk, qk, vt, subln_g)


def _pool_kernel(prev_ref, x_ref, next_ref, w_ref, s_ref, o_ref, *, seq, tm):
    i = pl.program_id(1)
    n = tm + 2 * POOL_HALO
    t = i * tm + lax.broadcasted_iota(jnp.int32, (tm, 1), 0)
    first = i == 0
    last = i == pl.num_programs(1) - 1
    for g, w in enumerate(POOL_WINDOWS):
        cols = slice(g * POOL_GROUP_DIM, (g + 1) * POOL_GROUP_DIM)
        x = x_ref[:, cols]
        prev = jnp.where(first, 0.0, prev_ref[:, cols])
        nxt = jnp.where(last, 0.0, next_ref[:, cols])
        a = jnp.concatenate([prev, x, nxt], axis=0)
        span = 1
        while span < w:
            a = a + pltpu.roll(a, n - span, axis=0)
            span *= 2
        off = POOL_HALO - w // 2
        if off:
            a = pltpu.roll(a, n - off, axis=0)
        wsum = a[:tm]
        lo = jnp.maximum(t - w // 2, 0)
        hi = jnp.minimum(t + w // 2 - 1, seq - 1)
        pooled = wsum / (hi - lo + 1).astype(F32) - x
        y = jnp.dot(pooled.astype(BF16), w_ref[g], preferred_element_type=F32)
        o_ref[:, cols] = (y * s_ref[:, cols]).astype(o_ref.dtype)


def _pool(rest, w_grp, scale, *, batch, seq, col_block, tm=512):
    nt = seq // tm
    hb = tm // POOL_HALO
    n_hblk = seq // POOL_HALO
    return pl.pallas_call(
        functools.partial(_pool_kernel, seq=seq, tm=tm),
        grid=(batch, nt),
        in_specs=[
            pl.BlockSpec((POOL_HALO, D_POOL),
                         lambda b, i: (b * n_hblk + jnp.maximum(i * hb - 1, 0), col_block)),
            pl.BlockSpec((tm, D_POOL), lambda b, i: (b * nt + i, col_block)),
            pl.BlockSpec((POOL_HALO, D_POOL),
                         lambda b, i: (b * n_hblk + jnp.minimum((i + 1) * hb, n_hblk - 1), col_block)),
            pl.BlockSpec((len(POOL_WINDOWS), POOL_GROUP_DIM, POOL_GROUP_DIM), lambda b, i: (0, 0, 0)),
            pl.BlockSpec((1, D_POOL), lambda b, i: (0, 0)),
        ],
        out_specs=pl.BlockSpec((tm, D_POOL), lambda b, i: (b * nt + i, 0)),
        out_shape=jax.ShapeDtypeStruct((batch * seq, D_POOL), BF16),
        compiler_params=_params(("parallel", "arbitrary")),
        name="pool_mixer",
    )(rest, rest, rest, w_grp, scale)


def _mix_kernel(at_ref, y_ref, ga_ref, gp_ref, h_ref, wa_ref, wp_ref, wo_ref, o_ref):
    a = jnp.dot(at_ref[...], wa_ref[...], preferred_element_type=F32)
    p = jnp.dot(y_ref[...], wp_ref[...], preferred_element_type=F32)
    merged = jax.nn.sigmoid(ga_ref[...]) * a + jax.nn.sigmoid(gp_ref[...]) * p
    o_ref[...] = h_ref[...] + jnp.dot(merged.astype(BF16), wo_ref[...], preferred_element_type=F32)


def _mix(attn, y, rest, h, w_a, w_p, w_o, *, tm=256):
    T = h.shape[0]
    const = lambda i: (0, 0)
    return pl.pallas_call(
        _mix_kernel,
        grid=(T // tm,),
        in_specs=[
            pl.BlockSpec((tm, D_ATTN), lambda i: (i, 0)),
            pl.BlockSpec((tm, D_POOL), lambda i: (i, 0)),
            pl.BlockSpec((tm, D_MODEL), lambda i: (i, 0)),
            pl.BlockSpec((tm, D_MODEL), lambda i: (i, 1)),
            pl.BlockSpec((tm, D_MODEL), lambda i: (i, 0)),
            pl.BlockSpec((D_ATTN, D_MODEL), const, pipeline_mode=pl.Buffered(1)),
            pl.BlockSpec((D_POOL, D_MODEL), const, pipeline_mode=pl.Buffered(1)),
            pl.BlockSpec((D_MODEL, D_MODEL), const, pipeline_mode=pl.Buffered(1)),
        ],
        out_specs=pl.BlockSpec((tm, D_MODEL), lambda i: (i, 0)),
        out_shape=jax.ShapeDtypeStruct((T, D_MODEL), F32),
        compiler_params=_params(("parallel",)),
        name="gated_mix",
    )(attn, y, rest, rest, h, w_a, w_p, w_o)


def _trunk(x, p):
    batch, seq, _ = x.shape
    xf = x.reshape(batch * seq, D_MODEL)
    h, u = _ffn(xf, p["ffn1_norm"], p["ffn1_w_gu"], p["ffn1_w_down"], p["mix_norm"], emit_u=True)
    qk = _mm_nn(u, p["w_qk"], BF16, name="proj_qk")
    vt = _mm_nt(p["w_v_t"], u, BF16, name="proj_vt")
    rest = _mm_nn(u, p["w_rest"], F32, name="proj_rest")
    attn = _attention(qk, vt, p["slopes"], p["lq1"], p["lk1"], p["lq2"], p["lk2"], p["subln_g"],
                      batch=batch, seq=seq)
    y = _pool(rest, p["w_pool_grp"], p["pool_scale"], batch=batch, seq=seq,
              col_block=2 * D_MODEL // D_POOL)
    h2 = _mix(attn, y, rest, h, p["w_attn_proj"], p["w_pool_proj"], p["w_out"])
    out = _ffn(h2, p["ffn2_norm"], p["ffn2_w_gu"], p["ffn2_w_down"], p["final_norm"], emit_u=False)
    return out.reshape(batch, seq, D_MODEL)


def kernel(x_prompt, x_sample, ffn1_norm, ffn1_w_gu, ffn1_w_down, mix_norm, w_in, lambda_q1, lambda_k1, lambda_q2, lambda_k2, attn_subln_g, w_attn_proj, w_pool_grp, pool_scale, w_pool_proj, w_out, ffn2_norm, ffn2_w_gu, ffn2_w_down, final_norm):
    l = 0
    w = w_in[l]
    c_v, c_p, c_ga = 2 * D_Q, 2 * D_Q + D_ATTN, 2 * D_Q + D_ATTN + D_POOL
    p = {
        "ffn1_norm": ffn1_norm[l][None], "mix_norm": mix_norm[l][None],
        "ffn2_norm": ffn2_norm[l][None], "final_norm": final_norm[None],
        "ffn1_w_gu": ffn1_w_gu[l].astype(BF16), "ffn1_w_down": ffn1_w_down[l].astype(BF16),
        "ffn2_w_gu": ffn2_w_gu[l].astype(BF16), "ffn2_w_down": ffn2_w_down[l].astype(BF16),
        "w_qk": w[:, :c_v].astype(BF16),
        "w_v_t": w[:, c_v:c_p].T.astype(BF16),
        "w_rest": jnp.concatenate([w[:, c_ga:], w[:, c_p:c_ga]], axis=1).astype(BF16),
        "slopes": jnp.asarray(2.0 ** (-8.0 * np.arange(1, N_HEADS + 1) / N_HEADS), dtype=F32),
        "lq1": lambda_q1[l][None], "lk1": lambda_k1[l][None],
        "lq2": lambda_q2[l][None], "lk2": lambda_k2[l][None],
        "subln_g": attn_subln_g[l][None],
        "w_attn_proj": w_attn_proj[l].astype(BF16),
        "w_pool_grp": w_pool_grp[l].astype(BF16),
        "pool_scale": pool_scale[l][None],
        "w_pool_proj": w_pool_proj[l].astype(BF16),
        "w_out": w_out[l].astype(BF16),
    }
    return (_trunk(x_prompt, p), _trunk(x_sample, p))
```

```python
import functools

import jax
import jax.numpy as jnp
import numpy as np
from jax import lax
from jax.experimental import pallas as pl
from jax.experimental.pallas import tpu as pltpu

D_MODEL = 2048
N_HEADS = 8
QK_DIM = 64
V_DIM = 2 * QK_DIM
HEAD_W = 2 * QK_DIM
D_ATTN = N_HEADS * V_DIM
D_Q = N_HEADS * HEAD_W
ALIBI_SPLIT = 64
BIAS_GROUP = 8
ONES_ROWS = 16
LOG2E = 1.4426950408889634
POOL_WINDOWS = (2, 4, 8, 16)
POOL_GROUP_DIM = 256
D_POOL = len(POOL_WINDOWS) * POOL_GROUP_DIM
POOL_HALO = 8
D_FF = 5632
EPS = 1e-6
LAMBDA_INIT = 0.8 - 0.6 * float(np.exp(-0.3 * 0))

V7X_VMEM_BYTES = 64 * 1024 * 1024
V7X_VMEM_LIMIT_BYTES = V7X_VMEM_BYTES - 8 * 1024 * 1024

F32 = jnp.float32
BF16 = jnp.bfloat16


def _params(semantics):
    return pltpu.CompilerParams(dimension_semantics=semantics,
                                vmem_limit_bytes=V7X_VMEM_LIMIT_BYTES)


def _rms(xf, g):
    return xf * lax.rsqrt(jnp.mean(xf * xf, axis=-1, keepdims=True) + EPS) * g


def _ffn_kernel(x_ref, g_ref, wg_ref, wu_ref, wd_ref, g2_ref, *refs, nf, emit_u):
    if emit_u:
        o_ref, u_ref, xn_ref, acc_ref = refs
    else:
        o_ref, xn_ref, acc_ref = refs
    f = pl.program_id(1)

    @pl.when(f == 0)
    def _():
        xn_ref[...] = _rms(x_ref[...], g_ref[...]).astype(BF16)
        acc_ref[...] = jnp.zeros_like(acc_ref)

    xn = xn_ref[...]
    gate = jnp.dot(xn, wg_ref[...], preferred_element_type=F32)
    up = jnp.dot(xn, wu_ref[...], preferred_element_type=F32)
    act = (gate * jax.nn.sigmoid(gate) * up).astype(BF16)
    acc_ref[...] += jnp.dot(act, wd_ref[...], preferred_element_type=F32)

    @pl.when(f == nf - 1)
    def _():
        h = x_ref[...] + 0.5 * acc_ref[...]
        if emit_u:
            o_ref[...] = h
            u_ref[...] = _rms(h, g2_ref[...]).astype(BF16)
        else:
            o_ref[...] = _rms(h, g2_ref[...])


def _ffn(x, g, w_gu, w_down, g2, *, emit_u, tm=512, tf=512):
    T = x.shape[0]
    nf = D_FF // tf
    grid = (T // tm, nf)
    in_specs = [
        pl.BlockSpec((tm, D_MODEL), lambda i, f: (i, 0)),
        pl.BlockSpec((1, D_MODEL), lambda i, f: (0, 0)),
        pl.BlockSpec((D_MODEL, tf), lambda i, f: (0, f)),
        pl.BlockSpec((D_MODEL, tf), lambda i, f: (0, nf + f)),
        pl.BlockSpec((tf, D_MODEL), lambda i, f: (f, 0)),
        pl.BlockSpec((1, D_MODEL), lambda i, f: (0, 0)),
    ]
    row_spec = pl.BlockSpec((tm, D_MODEL), lambda i, f: (i, 0))
    if emit_u:
        out_shape = (jax.ShapeDtypeStruct((T, D_MODEL), F32), jax.ShapeDtypeStruct((T, D_MODEL), BF16))
        out_specs = (row_spec, row_spec)
    else:
        out_shape = jax.ShapeDtypeStruct((T, D_MODEL), F32)
        out_specs = row_spec
    return pl.pallas_call(
        functools.partial(_ffn_kernel, nf=nf, emit_u=emit_u),
        grid=grid, in_specs=in_specs, out_specs=out_specs, out_shape=out_shape,
        scratch_shapes=[pltpu.VMEM((tm, D_MODEL), BF16), pltpu.VMEM((tm, D_MODEL), F32)],
        compiler_params=_params(("parallel", "arbitrary")),
        name="ffn_u" if emit_u else "ffn_final",
    )(x, g, w_gu, w_gu, w_down, g2)


def _mm_nn_kernel(x_ref, w_ref, o_ref):
    o_ref[...] = jnp.dot(x_ref[...], w_ref[...], preferred_element_type=F32).astype(o_ref.dtype)


def _mm_nn(x, w, out_dtype, *, tm=1024, tn=1024, name):
    T, K = x.shape
    N = w.shape[1]
    return pl.pallas_call(
        _mm_nn_kernel,
        grid=(T // tm, N // tn),
        in_specs=[pl.BlockSpec((tm, K), lambda i, j: (i, 0)),
                  pl.BlockSpec((K, tn), lambda i, j: (0, j))],
        out_specs=pl.BlockSpec((tm, tn), lambda i, j: (i, j)),
        out_shape=jax.ShapeDtypeStruct((T, N), out_dtype),
        compiler_params=_params(("parallel", "arbitrary")),
        name=name,
    )(x, w)


def _mm_nt_kernel(wt_ref, x_ref, o_ref):
    o_ref[...] = lax.dot_general(wt_ref[...], x_ref[...], (((1,), (1,)), ((), ())),
                                 preferred_element_type=F32).astype(o_ref.dtype)


def _mm_nt(wt, x, out_dtype, *, tm=1024, name):
    N, K = wt.shape
    T = x.shape[0]
    return pl.pallas_call(
        _mm_nt_kernel,
        grid=(T // tm,),
        in_specs=[pl.BlockSpec((N, K), lambda i: (0, 0)),
                  pl.BlockSpec((tm, K), lambda i: (i, 0))],
        out_specs=pl.BlockSpec((N, tm), lambda i: (0, i)),
        out_shape=jax.ShapeDtypeStruct((N, T), out_dtype),
        compiler_params=_params(("parallel",)),
        name=name,
    )(wt, x)


def _attn_kernel(slope_ref, lq1_ref, lk1_ref, lq2_ref, lk2_ref, q_ref, k_ref, vt_ref, g_ref, o_ref,
                 ka_ref, vta_ref, corr_ref, s_ref, *, seq, t):
    h = pl.program_id(1)
    i = pl.program_id(2)
    n = seq // t
    c2 = slope_ref[h] * LOG2E

    @pl.when(i == 0)
    def _():
        j = lax.broadcasted_iota(jnp.int32, (seq, HEAD_W), 0)
        lane = lax.broadcasted_iota(jnp.int32, (seq, HEAD_W), 1)
        sub = lane & (BIAS_GROUP - 1)
        j_lo = j & (ALIBI_SPLIT - 1)
        pos = jnp.where(lane < 3 * BIAS_GROUP,
                        jnp.where(sub == 0, j_lo, jnp.where(sub == 1, j - j_lo, jnp.where(sub == 2, 1, 0))), 0)
        pos = pos.astype(F32).astype(BF16)
        kf = k_ref[...].astype(F32)
        for mp in range(2):
            km = kf[:, mp * QK_DIM:(mp + 1) * QK_DIM]
            ka_ref[mp, :, :HEAD_W] = jnp.concatenate([km, km], axis=1).astype(BF16)
            ka_ref[mp, :, HEAD_W:] = pos
        vta_ref[:V_DIM, :] = vt_ref[...]
        vta_ref[V_DIM:, :] = jnp.ones((ONES_ROWS, seq), BF16)
        d = lax.broadcasted_iota(jnp.int32, (t, t), 0) - lax.broadcasted_iota(jnp.int32, (t, t), 1)
        corr_ref[:t, :] = (-2.0 * c2) * jnp.maximum(d, 0).astype(F32)
        corr_ref[t:, :] = jnp.zeros((t, t), F32)

    qa = q_ref[...].astype(F32).T * (QK_DIM ** -0.5 * LOG2E)
    q_hi = qa.astype(BF16)
    q_lo = (qa - q_hi.astype(F32)).astype(BF16)
    mains = [jnp.concatenate([q_hi[mp * QK_DIM:(mp + 1) * QK_DIM], q_lo[mp * QK_DIM:(mp + 1) * QK_DIM]], axis=0)
             for mp in range(2)]

    row = lax.broadcasted_iota(jnp.int32, (4 * BIAS_GROUP, t), 0)
    col = lax.broadcasted_iota(jnp.int32, (4 * BIAS_GROUP, t), 1)
    sub = row & (BIAS_GROUP - 1)
    q0c = c2 * (i * t).astype(F32)

    def bias_rows(base):
        hi = base.astype(BF16).astype(F32)
        mid = (base - hi).astype(BF16).astype(F32)
        lo = base - hi - mid
        b = jnp.where(row < BIAS_GROUP, hi, jnp.where(row < 2 * BIAS_GROUP, mid,
                                                      jnp.where(row < 3 * BIAS_GROUP, lo, 0.0)))
        return jnp.concatenate([b, jnp.zeros((HEAD_W - 4 * BIAS_GROUP, t), F32)], axis=0).astype(BF16)

    bias_l = bias_rows(jnp.where(sub < 2, c2, jnp.where(sub == 2, -q0c, 0.0)))
    bias_r = bias_rows(jnp.where(sub < 2, -c2, jnp.where(sub == 2, q0c + (2.0 * c2) * col.astype(F32), 0.0)))

    def score_step(ci, ms):
        r0 = pl.multiple_of(ci * t, t)
        bias = jnp.where(ci <= i, bias_l, bias_r)
        corr = corr_ref[pl.ds(pl.multiple_of(jnp.where(ci == i, 0, t), t), t), :]
        new = []
        for mp in range(2):
            q_aug = jnp.concatenate([mains[mp], bias], axis=0)
            s = jnp.dot(ka_ref[mp, pl.ds(r0, t), :], q_aug, preferred_element_type=F32) + corr
            s_ref[pl.ds(r0, t), mp * t:(mp + 1) * t] = s
            new.append(jnp.maximum(ms[mp], jnp.max(s, axis=0, keepdims=True)))
        return tuple(new)

    m_init = jnp.full((1, t), -1e30, F32)
    ms = lax.fori_loop(0, n, score_step, (m_init, m_init), unroll=True)
    m = jnp.concatenate(ms, axis=1)

    def value_step(ci, acc):
        r0 = pl.multiple_of(ci * t, t)
        e = jnp.exp2(s_ref[pl.ds(r0, t), :] - m).astype(BF16)
        return acc + jnp.dot(vta_ref[:, pl.ds(r0, t)], e, preferred_element_type=F32)

    acc = lax.fori_loop(0, n, value_step, jnp.zeros((V_DIM + ONES_ROWS, 2 * t), F32),
                        unroll=max(n // 2, 1))
    out = acc[:V_DIM] * (1.0 / acc[V_DIM:V_DIM + 1])

    lam = (jnp.exp(jnp.sum(lq1_ref[...] * lk1_ref[...], axis=-1, keepdims=True))
           - jnp.exp(jnp.sum(lq2_ref[...] * lk2_ref[...], axis=-1, keepdims=True)) + LAMBDA_INIT)
    o = (out[:, :t] - lam * out[:, t:]).T
    o_ref[...] = (_rms(o, g_ref[...]) * (1.0 - LAMBDA_INIT)).astype(o_ref.dtype)


def _attn_tile(seq):
    for t in (1024, 512, 256, 128):
        if seq % t == 0 and seq * 2 * t * 4 <= V7X_VMEM_BYTES // 4:
            return t
    raise ValueError(f"no attention tile for sequence length {seq}")


def _attention(qk, vt, slopes, lq1, lk1, lq2, lk2, subln_g, *, batch, seq, t):
    nq = seq // t
    vec = pl.BlockSpec((1, QK_DIM), lambda b, h, i: (0, 0))
    return pl.pallas_call(
        functools.partial(_attn_kernel, seq=seq, t=t),
        grid=(batch, N_HEADS, nq),
        in_specs=[
            pl.BlockSpec(memory_space=pltpu.SMEM),
            vec, vec, vec, vec,
            pl.BlockSpec((t, HEAD_W), lambda b, h, i: (b * nq + i, h)),
            pl.BlockSpec((seq, HEAD_W), lambda b, h, i: (b, N_HEADS + h)),
            pl.BlockSpec((V_DIM, seq), lambda b, h, i: (h, b)),
            pl.BlockSpec((1, V_DIM), lambda b, h, i: (0, 0)),
        ],
        out_specs=pl.BlockSpec((t, V_DIM), lambda b, h, i: (b * nq + i, h)),
        out_shape=jax.ShapeDtypeStruct((batch * seq, D_ATTN), BF16),
        scratch_shapes=[
            pltpu.VMEM((2, seq, 2 * HEAD_W), BF16),
            pltpu.VMEM((V_DIM + ONES_ROWS, seq), BF16),
            pltpu.VMEM((2 * t, t), F32),
            pltpu.VMEM((seq, 2 * t), F32),
        ],
        compiler_params=_params(("parallel", "parallel", "arbitrary")),
        name="diff_attn",
    )(slopes, lq1, lk1, lq2, lk2, qk, qk, vt, subln_g)


def _pool_kernel(prev_ref, x_ref, next_ref, w_ref, s_ref, o_ref, *, seq, tm):
    i = pl.program_id(1)
    n = tm + 2 * POOL_HALO
    t = i * tm + lax.broadcasted_iota(jnp.int32, (tm, 1), 0)
    first = i == 0
    last = i == pl.num_programs(1) - 1
    for g, w in enumerate(POOL_WINDOWS):
        cols = slice(g * POOL_GROUP_DIM, (g + 1) * POOL_GROUP_DIM)
        x = x_ref[:, cols]
        prev = jnp.where(first, 0.0, prev_ref[:, cols])
        nxt = jnp.where(last, 0.0, next_ref[:, cols])
        a = jnp.concatenate([prev, x, nxt], axis=0)
        span = 1
        while span < w:
            a = a + pltpu.roll(a, n - span, axis=0)
            span *= 2
        off = POOL_HALO - w // 2
        if off:
            a = pltpu.roll(a, n - off, axis=0)
        wsum = a[:tm]
        lo = jnp.maximum(t - w // 2, 0)
        hi = jnp.minimum(t + w // 2 - 1, seq - 1)
        pooled = wsum / (hi - lo + 1).astype(F32) - x
        y = jnp.dot(pooled.astype(BF16), w_ref[g], preferred_element_type=F32)
        o_ref[:, cols] = (y * s_ref[:, cols]).astype(o_ref.dtype)


def _pool(rest, w_grp, scale, *, batch, seq, col_block, tm=512):
    nt = seq // tm
    hb = tm // POOL_HALO
    n_hblk = seq // POOL_HALO
    return pl.pallas_call(
        functools.partial(_pool_kernel, seq=seq, tm=tm),
        grid=(batch, nt),
        in_specs=[
            pl.BlockSpec((POOL_HALO, D_POOL),
                         lambda b, i: (b * n_hblk + jnp.maximum(i * hb - 1, 0), col_block)),
            pl.BlockSpec((tm, D_POOL), lambda b, i: (b * nt + i, col_block)),
            pl.BlockSpec((POOL_HALO, D_POOL),
                         lambda b, i: (b * n_hblk + jnp.minimum((i + 1) * hb, n_hblk - 1), col_block)),
            pl.BlockSpec((len(POOL_WINDOWS), POOL_GROUP_DIM, POOL_GROUP_DIM), lambda b, i: (0, 0, 0)),
            pl.BlockSpec((1, D_POOL), lambda b, i: (0, 0)),
        ],
        out_specs=pl.BlockSpec((tm, D_POOL), lambda b, i: (b * nt + i, 0)),
        out_shape=jax.ShapeDtypeStruct((batch * seq, D_POOL), BF16),
        compiler_params=_params(("parallel", "arbitrary")),
        name="pool_mixer",
    )(rest, rest, rest, w_grp, scale)


def _mix_kernel(at_ref, y_ref, ga_ref, gp_ref, h_ref, wa_ref, wp_ref, wo_ref, o_ref):
    a = jnp.dot(at_ref[...], wa_ref[...], preferred_element_type=F32)
    p = jnp.dot(y_ref[...], wp_ref[...], preferred_element_type=F32)
    merged = jax.nn.sigmoid(ga_ref[...]) * a + jax.nn.sigmoid(gp_ref[...]) * p
    o_ref[...] = h_ref[...] + jnp.dot(merged.astype(BF16), wo_ref[...], preferred_element_type=F32)


def _mix(attn, y, rest, h, w_a, w_p, w_o, *, tm=256):
    T = h.shape[0]
    const = lambda i: (0, 0)
    return pl.pallas_call(
        _mix_kernel,
        grid=(T // tm,),
        in_specs=[
            pl.BlockSpec((tm, D_ATTN), lambda i: (i, 0)),
            pl.BlockSpec((tm, D_POOL), lambda i: (i, 0)),
            pl.BlockSpec((tm, D_MODEL), lambda i: (i, 0)),
            pl.BlockSpec((tm, D_MODEL), lambda i: (i, 1)),
            pl.BlockSpec((tm, D_MODEL), lambda i: (i, 0)),
            pl.BlockSpec((D_ATTN, D_MODEL), const, pipeline_mode=pl.Buffered(1)),
            pl.BlockSpec((D_POOL, D_MODEL), const, pipeline_mode=pl.Buffered(1)),
            pl.BlockSpec((D_MODEL, D_MODEL), const, pipeline_mode=pl.Buffered(1)),
        ],
        out_specs=pl.BlockSpec((tm, D_MODEL), lambda i: (i, 0)),
        out_shape=jax.ShapeDtypeStruct((T, D_MODEL), F32),
        compiler_params=_params(("parallel",)),
        name="gated_mix",
    )(attn, y, rest, rest, h, w_a, w_p, w_o)


def _trunk(x, p):
    batch, seq, _ = x.shape
    xf = x.reshape(batch * seq, D_MODEL)
    h, u = _ffn(xf, p["ffn1_norm"], p["ffn1_w_gu"], p["ffn1_w_down"], p["mix_norm"], emit_u=True)
    qk = _mm_nn(u, p["w_qk"], BF16, name="proj_qk")
    vt = _mm_nt(p["w_v_t"], u, BF16, name="proj_vt")
    rest = _mm_nn(u, p["w_rest"], F32, name="proj_rest")
    attn = _attention(qk, vt, p["slopes"], p["lq1"], p["lk1"], p["lq2"], p["lk2"], p["subln_g"],
                      batch=batch, seq=seq, t=_attn_tile(seq))
    y = _pool(rest, p["w_pool_grp"], p["pool_scale"], batch=batch, seq=seq,
              col_block=2 * D_MODEL // D_POOL)
    h2 = _mix(attn, y, rest, h, p["w_attn_proj"], p["w_pool_proj"], p["w_out"])
    out = _ffn(h2, p["ffn2_norm"], p["ffn2_w_gu"], p["ffn2_w_down"], p["final_norm"], emit_u=False)
    return out.reshape(batch, seq, D_MODEL)


def kernel(x_prompt, x_sample, ffn1_norm, ffn1_w_gu, ffn1_w_down, mix_norm, w_in, lambda_q1, lambda_k1, lambda_q2, lambda_k2, attn_subln_g, w_attn_proj, w_pool_grp, pool_scale, w_pool_proj, w_out, ffn2_norm, ffn2_w_gu, ffn2_w_down, final_norm):
    l = 0
    w = w_in[l]
    c_v, c_p, c_ga = 2 * D_Q, 2 * D_Q + D_ATTN, 2 * D_Q + D_ATTN + D_POOL
    p = {
        "ffn1_norm": ffn1_norm[l][None], "mix_norm": mix_norm[l][None],
        "ffn2_norm": ffn2_norm[l][None], "final_norm": final_norm[None],
        "ffn1_w_gu": ffn1_w_gu[l].astype(BF16), "ffn1_w_down": ffn1_w_down[l].astype(BF16),
        "ffn2_w_gu": ffn2_w_gu[l].astype(BF16), "ffn2_w_down": ffn2_w_down[l].astype(BF16),
        "w_qk": w[:, :c_v].astype(BF16),
        "w_v_t": w[:, c_v:c_p].T.astype(BF16),
        "w_rest": jnp.concatenate([w[:, c_ga:], w[:, c_p:c_ga]], axis=1).astype(BF16),
        "slopes": jnp.asarray(2.0 ** (-8.0 * np.arange(1, N_HEADS + 1) / N_HEADS), dtype=F32),
        "lq1": lambda_q1[l][None], "lk1": lambda_k1[l][None],
        "lq2": lambda_q2[l][None], "lk2": lambda_k2[l][None],
        "subln_g": attn_subln_g[l][None],
        "w_attn_proj": w_attn_proj[l].astype(BF16),
        "w_pool_grp": w_pool_grp[l].astype(BF16),
        "pool_scale": pool_scale[l][None],
        "w_pool_proj": w_pool_proj[l].astype(BF16),
        "w_out": w_out[l].astype(BF16),
    }
    return (_trunk(x_prompt, p), _trunk(x_sample, p))
```

```python
import functools

import jax
import jax.numpy as jnp
import numpy as np
from jax import lax
from jax.experimental import pallas as pl
from jax.experimental.pallas import tpu as pltpu

D_MODEL = 2048
N_HEADS = 8
QK_DIM = 64
V_DIM = 2 * QK_DIM
HEAD_W = 2 * QK_DIM
D_ATTN = N_HEADS * V_DIM
D_Q = N_HEADS * HEAD_W
ALIBI_SPLIT = 64
BIAS_GROUP = 8
ONES_ROWS = 16
LOG2E = 1.4426950408889634
POOL_WINDOWS = (2, 4, 8, 16)
POOL_GROUP_DIM = 256
D_POOL = len(POOL_WINDOWS) * POOL_GROUP_DIM
POOL_HALO = 8
D_FF = 5632
EPS = 1e-6
LAMBDA_INIT = 0.8 - 0.6 * float(np.exp(-0.3 * 0))

V7X_VMEM_BYTES = 64 * 1024 * 1024
V7X_VMEM_LIMIT_BYTES = V7X_VMEM_BYTES - 8 * 1024 * 1024
V7X_VMEM_LIMIT_MAX_BYTES = V7X_VMEM_BYTES - 512 * 1024
FFN1_ROWS = 512
FFN2_ROWS = 1024

F32 = jnp.float32
BF16 = jnp.bfloat16


def _params(semantics, vmem_limit_bytes=V7X_VMEM_LIMIT_BYTES):
    return pltpu.CompilerParams(dimension_semantics=semantics, vmem_limit_bytes=vmem_limit_bytes)


def _rms(xf, g):
    return xf * lax.rsqrt(jnp.mean(xf * xf, axis=-1, keepdims=True) + EPS) * g


def _ffn_kernel(x_ref, g_ref, wg_ref, wu_ref, wd_ref, g2_ref, *refs, nf, emit_u):
    if emit_u:
        o_ref, u_ref, xn_ref = refs
    else:
        o_ref, xn_ref = refs
    f = pl.program_id(1)

    @pl.when(f == 0)
    def _():
        xn_ref[...] = _rms(x_ref[...], g_ref[...]).astype(BF16)
        o_ref[...] = jnp.zeros_like(o_ref)

    xn = xn_ref[...]
    gate = jnp.dot(xn, wg_ref[...], preferred_element_type=F32)
    up = jnp.dot(xn, wu_ref[...], preferred_element_type=F32)
    act = (gate * jax.nn.sigmoid(gate) * up).astype(BF16)
    o_ref[...] += jnp.dot(act, wd_ref[...], preferred_element_type=F32)

    @pl.when(f == nf - 1)
    def _():
        h = x_ref[...] + 0.5 * o_ref[...]
        if emit_u:
            o_ref[...] = h
            u_ref[...] = _rms(h, g2_ref[...]).astype(BF16)
        else:
            o_ref[...] = _rms(h, g2_ref[...])


def _ffn(x, g, w_gu, w_down, g2, *, emit_u, tm, tf=512):
    T = x.shape[0]
    nf = D_FF // tf
    grid = (T // tm, nf)
    in_specs = [
        pl.BlockSpec((tm, D_MODEL), lambda i, f: (i, 0)),
        pl.BlockSpec((1, D_MODEL), lambda i, f: (0, 0)),
        pl.BlockSpec((D_MODEL, tf), lambda i, f: (0, f)),
        pl.BlockSpec((D_MODEL, tf), lambda i, f: (0, nf + f)),
        pl.BlockSpec((tf, D_MODEL), lambda i, f: (f, 0)),
        pl.BlockSpec((1, D_MODEL), lambda i, f: (0, 0)),
    ]
    row_spec = pl.BlockSpec((tm, D_MODEL), lambda i, f: (i, 0))
    if emit_u:
        out_shape = (jax.ShapeDtypeStruct((T, D_MODEL), F32), jax.ShapeDtypeStruct((T, D_MODEL), BF16))
        out_specs = (row_spec, row_spec)
    else:
        out_shape = jax.ShapeDtypeStruct((T, D_MODEL), F32)
        out_specs = row_spec
    return pl.pallas_call(
        functools.partial(_ffn_kernel, nf=nf, emit_u=emit_u),
        grid=grid, in_specs=in_specs, out_specs=out_specs, out_shape=out_shape,
        scratch_shapes=[pltpu.VMEM((tm, D_MODEL), BF16)],
        compiler_params=_params(("parallel", "arbitrary"),
                                V7X_VMEM_LIMIT_BYTES if tm <= FFN1_ROWS else V7X_VMEM_LIMIT_MAX_BYTES),
        name="ffn_u" if emit_u else "ffn_final",
    )(x, g, w_gu, w_gu, w_down, g2)


def _mm_nn_kernel(x_ref, w_ref, o_ref):
    o_ref[...] = jnp.dot(x_ref[...], w_ref[...], preferred_element_type=F32).astype(o_ref.dtype)


def _mm_nn(x, w, out_dtype, *, tm=1024, tn=1024, name):
    T, K = x.shape
    N = w.shape[1]
    return pl.pallas_call(
        _mm_nn_kernel,
        grid=(T // tm, N // tn),
        in_specs=[pl.BlockSpec((tm, K), lambda i, j: (i, 0)),
                  pl.BlockSpec((K, tn), lambda i, j: (0, j))],
        out_specs=pl.BlockSpec((tm, tn), lambda i, j: (i, j)),
        out_shape=jax.ShapeDtypeStruct((T, N), out_dtype),
        compiler_params=_params(("parallel", "arbitrary")),
        name=name,
    )(x, w)


def _mm_nt_kernel(wt_ref, x_ref, o_ref):
    o_ref[...] = lax.dot_general(wt_ref[...], x_ref[...], (((1,), (1,)), ((), ())),
                                 preferred_element_type=F32).astype(o_ref.dtype)


def _mm_nt(wt, x, out_dtype, *, tm=1024, name):
    N, K = wt.shape
    T = x.shape[0]
    return pl.pallas_call(
        _mm_nt_kernel,
        grid=(T // tm,),
        in_specs=[pl.BlockSpec((N, K), lambda i: (0, 0)),
                  pl.BlockSpec((tm, K), lambda i: (i, 0))],
        out_specs=pl.BlockSpec((N, tm), lambda i: (0, i)),
        out_shape=jax.ShapeDtypeStruct((N, T), out_dtype),
        compiler_params=_params(("parallel",)),
        name=name,
    )(wt, x)


def _attn_kernel(slope_ref, lq1_ref, lk1_ref, lq2_ref, lk2_ref, q_ref, k_ref, vt_ref, g_ref, o_ref,
                 ka_ref, vta_ref, corr_ref, s_ref, *, seq, t):
    h = pl.program_id(1)
    i = pl.program_id(2)
    n = seq // t
    c2 = slope_ref[h] * LOG2E

    @pl.when(i == 0)
    def _():
        j = lax.broadcasted_iota(jnp.int32, (seq, HEAD_W), 0)
        lane = lax.broadcasted_iota(jnp.int32, (seq, HEAD_W), 1)
        sub = lane & (BIAS_GROUP - 1)
        j_lo = j & (ALIBI_SPLIT - 1)
        pos = jnp.where(lane < 3 * BIAS_GROUP,
                        jnp.where(sub == 0, j_lo, jnp.where(sub == 1, j - j_lo, jnp.where(sub == 2, 1, 0))), 0)
        pos = pos.astype(F32).astype(BF16)
        kf = k_ref[...].astype(F32)
        for mp in range(2):
            km = kf[:, mp * QK_DIM:(mp + 1) * QK_DIM]
            ka_ref[mp, :, :HEAD_W] = jnp.concatenate([km, km], axis=1).astype(BF16)
            ka_ref[mp, :, HEAD_W:] = pos
        vta_ref[:V_DIM, :] = vt_ref[...]
        vta_ref[V_DIM:, :] = jnp.ones((ONES_ROWS, seq), BF16)
        d = lax.broadcasted_iota(jnp.int32, (t, t), 0) - lax.broadcasted_iota(jnp.int32, (t, t), 1)
        corr_ref[:t, :] = (-2.0 * c2) * jnp.maximum(d, 0).astype(F32)
        corr_ref[t:, :] = jnp.zeros((t, t), F32)

    qa = q_ref[...].astype(F32).T * (QK_DIM ** -0.5 * LOG2E)
    q_hi = qa.astype(BF16)
    q_lo = (qa - q_hi.astype(F32)).astype(BF16)
    mains = [jnp.concatenate([q_hi[mp * QK_DIM:(mp + 1) * QK_DIM], q_lo[mp * QK_DIM:(mp + 1) * QK_DIM]], axis=0)
             for mp in range(2)]

    row = lax.broadcasted_iota(jnp.int32, (4 * BIAS_GROUP, t), 0)
    col = lax.broadcasted_iota(jnp.int32, (4 * BIAS_GROUP, t), 1)
    sub = row & (BIAS_GROUP - 1)
    q0c = c2 * (i * t).astype(F32)

    def bias_rows(base):
        hi = base.astype(BF16).astype(F32)
        mid = (base - hi).astype(BF16).astype(F32)
        lo = base - hi - mid
        b = jnp.where(row < BIAS_GROUP, hi, jnp.where(row < 2 * BIAS_GROUP, mid,
                                                      jnp.where(row < 3 * BIAS_GROUP, lo, 0.0)))
        return jnp.concatenate([b, jnp.zeros((HEAD_W - 4 * BIAS_GROUP, t), F32)], axis=0).astype(BF16)

    bias_l = bias_rows(jnp.where(sub < 2, c2, jnp.where(sub == 2, -q0c, 0.0)))
    bias_r = bias_rows(jnp.where(sub < 2, -c2, jnp.where(sub == 2, q0c + (2.0 * c2) * col.astype(F32), 0.0)))

    def score_step(ci, ms):
        r0 = pl.multiple_of(ci * t, t)
        bias = jnp.where(ci <= i, bias_l, bias_r)
        corr = corr_ref[pl.ds(pl.multiple_of(jnp.where(ci == i, 0, t), t), t), :]
        new = []
        for mp in range(2):
            q_aug = jnp.concatenate([mains[mp], bias], axis=0)
            s = jnp.dot(ka_ref[mp, pl.ds(r0, t), :], q_aug, preferred_element_type=F32) + corr
            s_ref[pl.ds(r0, t), mp * t:(mp + 1) * t] = s
            new.append(jnp.maximum(ms[mp], jnp.max(s, axis=0, keepdims=True)))
        return tuple(new)

    m_init = jnp.full((1, t), -1e30, F32)
    ms = lax.fori_loop(0, n, score_step, (m_init, m_init), unroll=True)
    m = jnp.concatenate(ms, axis=1)

    def value_step(ci, acc):
        r0 = pl.multiple_of(ci * t, t)
        e = jnp.exp2(s_ref[pl.ds(r0, t), :] - m).astype(BF16)
        return acc + jnp.dot(vta_ref[:, pl.ds(r0, t)], e, preferred_element_type=F32)

    acc = lax.fori_loop(0, n, value_step, jnp.zeros((V_DIM + ONES_ROWS, 2 * t), F32),
                        unroll=max(n // 2, 1))
    out = acc[:V_DIM] * (1.0 / acc[V_DIM:V_DIM + 1])

    lam = (jnp.exp(jnp.sum(lq1_ref[...] * lk1_ref[...], axis=-1, keepdims=True))
           - jnp.exp(jnp.sum(lq2_ref[...] * lk2_ref[...], axis=-1, keepdims=True)) + LAMBDA_INIT)
    o = (out[:, :t] - lam * out[:, t:]).T
    o_ref[...] = (_rms(o, g_ref[...]) * (1.0 - LAMBDA_INIT)).astype(o_ref.dtype)


def _attn_tile(seq):
    for t in (1024, 512, 256, 128):
        if seq % t == 0 and seq * 2 * t * 4 <= V7X_VMEM_BYTES // 4:
            return t
    raise ValueError(f"no attention tile for sequence length {seq}")


def _attention(qk, vt, slopes, lq1, lk1, lq2, lk2, subln_g, *, batch, seq, t):
    nq = seq // t
    vec = pl.BlockSpec((1, QK_DIM), lambda b, h, i: (0, 0))
    return pl.pallas_call(
        functools.partial(_attn_kernel, seq=seq, t=t),
        grid=(batch, N_HEADS, nq),
        in_specs=[
            pl.BlockSpec(memory_space=pltpu.SMEM),
            vec, vec, vec, vec,
            pl.BlockSpec((t, HEAD_W), lambda b, h, i: (b * nq + i, h)),
            pl.BlockSpec((seq, HEAD_W), lambda b, h, i: (b, N_HEADS + h)),
            pl.BlockSpec((V_DIM, seq), lambda b, h, i: (h, b)),
            pl.BlockSpec((1, V_DIM), lambda b, h, i: (0, 0)),
        ],
        out_specs=pl.BlockSpec((t, V_DIM), lambda b, h, i: (b * nq + i, h)),
        out_shape=jax.ShapeDtypeStruct((batch * seq, D_ATTN), BF16),
        scratch_shapes=[
            pltpu.VMEM((2, seq, 2 * HEAD_W), BF16),
            pltpu.VMEM((V_DIM + ONES_ROWS, seq), BF16),
            pltpu.VMEM((2 * t, t), F32),
            pltpu.VMEM((seq, 2 * t), F32),
        ],
        compiler_params=_params(("parallel", "parallel", "arbitrary")),
        name="diff_attn",
    )(slopes, lq1, lk1, lq2, lk2, qk, qk, vt, subln_g)


def _pool_kernel(prev_ref, x_ref, next_ref, w_ref, s_ref, o_ref, *, seq, tm):
    i = pl.program_id(1)
    n = tm + 2 * POOL_HALO
    t = i * tm + lax.broadcasted_iota(jnp.int32, (tm, 1), 0)
    first = i == 0
    last = i == pl.num_programs(1) - 1
    for g, w in enumerate(POOL_WINDOWS):
        cols = slice(g * POOL_GROUP_DIM, (g + 1) * POOL_GROUP_DIM)
        x = x_ref[:, cols]
        prev = jnp.where(first, 0.0, prev_ref[:, cols])
        nxt = jnp.where(last, 0.0, next_ref[:, cols])
        a = jnp.concatenate([prev, x, nxt], axis=0)
        span = 1
        while span < w:
            a = a + pltpu.roll(a, n - span, axis=0)
            span *= 2
        off = POOL_HALO - w // 2
        if off:
            a = pltpu.roll(a, n - off, axis=0)
        wsum = a[:tm]
        lo = jnp.maximum(t - w // 2, 0)
        hi = jnp.minimum(t + w // 2 - 1, seq - 1)
        pooled = wsum / (hi - lo + 1).astype(F32) - x
        y = jnp.dot(pooled.astype(BF16), w_ref[g], preferred_element_type=F32)
        o_ref[:, cols] = (y * s_ref[:, cols]).astype(o_ref.dtype)


def _pool(rest, w_grp, scale, *, batch, seq, col_block, tm=512):
    nt = seq // tm
    hb = tm // POOL_HALO
    n_hblk = seq // POOL_HALO
    return pl.pallas_call(
        functools.partial(_pool_kernel, seq=seq, tm=tm),
        grid=(batch, nt),
        in_specs=[
            pl.BlockSpec((POOL_HALO, D_POOL),
                         lambda b, i: (b * n_hblk + jnp.maximum(i * hb - 1, 0), col_block)),
            pl.BlockSpec((tm, D_POOL), lambda b, i: (b * nt + i, col_block)),
            pl.BlockSpec((POOL_HALO, D_POOL),
                         lambda b, i: (b * n_hblk + jnp.minimum((i + 1) * hb, n_hblk - 1), col_block)),
            pl.BlockSpec((len(POOL_WINDOWS), POOL_GROUP_DIM, POOL_GROUP_DIM), lambda b, i: (0, 0, 0)),
            pl.BlockSpec((1, D_POOL), lambda b, i: (0, 0)),
        ],
        out_specs=pl.BlockSpec((tm, D_POOL), lambda b, i: (b * nt + i, 0)),
        out_shape=jax.ShapeDtypeStruct((batch * seq, D_POOL), BF16),
        compiler_params=_params(("parallel", "arbitrary")),
        name="pool_mixer",
    )(rest, rest, rest, w_grp, scale)


def _mix_kernel(at_ref, y_ref, ga_ref, gp_ref, h_ref, wa_ref, wp_ref, wo_ref, o_ref):
    a = jnp.dot(at_ref[...], wa_ref[...], preferred_element_type=F32)
    p = jnp.dot(y_ref[...], wp_ref[...], preferred_element_type=F32)
    merged = jax.nn.sigmoid(ga_ref[...]) * a + jax.nn.sigmoid(gp_ref[...]) * p
    o_ref[...] = h_ref[...] + jnp.dot(merged.astype(BF16), wo_ref[...], preferred_element_type=F32)


def _mix(attn, y, rest, h, w_a, w_p, w_o, *, tm=256):
    T = h.shape[0]
    const = lambda i: (0, 0)
    return pl.pallas_call(
        _mix_kernel,
        grid=(T // tm,),
        in_specs=[
            pl.BlockSpec((tm, D_ATTN), lambda i: (i, 0)),
            pl.BlockSpec((tm, D_POOL), lambda i: (i, 0)),
            pl.BlockSpec((tm, D_MODEL), lambda i: (i, 0)),
            pl.BlockSpec((tm, D_MODEL), lambda i: (i, 1)),
            pl.BlockSpec((tm, D_MODEL), lambda i: (i, 0)),
            pl.BlockSpec((D_ATTN, D_MODEL), const, pipeline_mode=pl.Buffered(1)),
            pl.BlockSpec((D_POOL, D_MODEL), const, pipeline_mode=pl.Buffered(1)),
            pl.BlockSpec((D_MODEL, D_MODEL), const, pipeline_mode=pl.Buffered(1)),
        ],
        out_specs=pl.BlockSpec((tm, D_MODEL), lambda i: (i, 0)),
        out_shape=jax.ShapeDtypeStruct((T, D_MODEL), F32),
        compiler_params=_params(("parallel",)),
        name="gated_mix",
    )(attn, y, rest, rest, h, w_a, w_p, w_o)


def _trunk(x, p):
    batch, seq, _ = x.shape
    xf = x.reshape(batch * seq, D_MODEL)
    h, u = _ffn(xf, p["ffn1_norm"], p["ffn1_w_gu"], p["ffn1_w_down"], p["mix_norm"], emit_u=True,
                tm=FFN1_ROWS)
    qk = _mm_nn(u, p["w_qk"], BF16, name="proj_qk")
    vt = _mm_nt(p["w_v_t"], u, BF16, name="proj_vt")
    rest = _mm_nn(u, p["w_rest"], F32, name="proj_rest")
    attn = _attention(qk, vt, p["slopes"], p["lq1"], p["lk1"], p["lq2"], p["lk2"], p["subln_g"],
                      batch=batch, seq=seq, t=_attn_tile(seq))
    y = _pool(rest, p["w_pool_grp"], p["pool_scale"], batch=batch, seq=seq,
              col_block=2 * D_MODEL // D_POOL)
    h2 = _mix(attn, y, rest, h, p["w_attn_proj"], p["w_pool_proj"], p["w_out"])
    out = _ffn(h2, p["ffn2_norm"], p["ffn2_w_gu"], p["ffn2_w_down"], p["final_norm"], emit_u=False,
               tm=FFN2_ROWS)
    return out.reshape(batch, seq, D_MODEL)


def kernel(x_prompt, x_sample, ffn1_norm, ffn1_w_gu, ffn1_w_down, mix_norm, w_in, lambda_q1, lambda_k1, lambda_q2, lambda_k2, attn_subln_g, w_attn_proj, w_pool_grp, pool_scale, w_pool_proj, w_out, ffn2_norm, ffn2_w_gu, ffn2_w_down, final_norm):
    l = 0
    w = w_in[l]
    c_v, c_p, c_ga = 2 * D_Q, 2 * D_Q + D_ATTN, 2 * D_Q + D_ATTN + D_POOL
    p = {
        "ffn1_norm": ffn1_norm[l][None], "mix_norm": mix_norm[l][None],
        "ffn2_norm": ffn2_norm[l][None], "final_norm": final_norm[None],
        "ffn1_w_gu": ffn1_w_gu[l].astype(BF16), "ffn1_w_down": ffn1_w_down[l].astype(BF16),
        "ffn2_w_gu": ffn2_w_gu[l].astype(BF16), "ffn2_w_down": ffn2_w_down[l].astype(BF16),
        "w_qk": w[:, :c_v].astype(BF16),
        "w_v_t": w[:, c_v:c_p].T.astype(BF16),
        "w_rest": jnp.concatenate([w[:, c_ga:], w[:, c_p:c_ga]], axis=1).astype(BF16),
        "slopes": jnp.asarray(2.0 ** (-8.0 * np.arange(1, N_HEADS + 1) / N_HEADS), dtype=F32),
        "lq1": lambda_q1[l][None], "lk1": lambda_k1[l][None],
        "lq2": lambda_q2[l][None], "lk2": lambda_k2[l][None],
        "subln_g": attn_subln_g[l][None],
        "w_attn_proj": w_attn_proj[l].astype(BF16),
        "w_pool_grp": w_pool_grp[l].astype(BF16),
        "pool_scale": pool_scale[l][None],
        "w_pool_proj": w_pool_proj[l].astype(BF16),
        "w_out": w_out[l].astype(BF16),
    }
    return (_trunk(x_prompt, p), _trunk(x_sample, p))
```

```python
import functools

import jax
import jax.numpy as jnp
import numpy as np
from jax import lax
from jax.experimental import pallas as pl
from jax.experimental.pallas import tpu as pltpu

D_MODEL = 2048
N_HEADS = 8
QK_DIM = 64
V_DIM = 2 * QK_DIM
HEAD_W = 2 * QK_DIM
D_ATTN = N_HEADS * V_DIM
D_Q = N_HEADS * HEAD_W
ALIBI_SPLIT = 64
BIAS_GROUP = 8
ONES_ROWS = 16
LOG2E = 1.4426950408889634
POOL_WINDOWS = (2, 4, 8, 16)
POOL_GROUP_DIM = 256
D_POOL = len(POOL_WINDOWS) * POOL_GROUP_DIM
POOL_HALO = 8
D_FF = 5632
EPS = 1e-6
LAMBDA_INIT = 0.8 - 0.6 * float(np.exp(-0.3 * 0))

V7X_VMEM_BYTES = 64 * 1024 * 1024
V7X_VMEM_LIMIT_BYTES = V7X_VMEM_BYTES - 8 * 1024 * 1024
V7X_VMEM_LIMIT_MAX_BYTES = V7X_VMEM_BYTES - 512 * 1024
FFN1_ROWS = 512
FFN2_ROWS = 1024

F32 = jnp.float32
BF16 = jnp.bfloat16


def _params(semantics, vmem_limit_bytes=V7X_VMEM_LIMIT_BYTES):
    return pltpu.CompilerParams(dimension_semantics=semantics, vmem_limit_bytes=vmem_limit_bytes)


def _rms(xf, g):
    return xf * lax.rsqrt(jnp.mean(xf * xf, axis=-1, keepdims=True) + EPS) * g


def _ffn_kernel(x_ref, g_ref, wg_ref, wu_ref, wd_ref, g2_ref, *refs, nf, emit_u):
    if emit_u:
        o_ref, u_ref, xn_ref = refs
    else:
        o_ref, xn_ref = refs
    f = pl.program_id(1)

    @pl.when(f == 0)
    def _():
        xn_ref[...] = _rms(x_ref[...], g_ref[...]).astype(BF16)
        o_ref[...] = jnp.zeros_like(o_ref)

    xn = xn_ref[...]
    gate = jnp.dot(xn, wg_ref[...], preferred_element_type=F32)
    up = jnp.dot(xn, wu_ref[...], preferred_element_type=F32)
    act = (gate * jax.nn.sigmoid(gate) * up).astype(BF16)
    o_ref[...] += jnp.dot(act, wd_ref[...], preferred_element_type=F32)

    @pl.when(f == nf - 1)
    def _():
        h = x_ref[...] + 0.5 * o_ref[...]
        if emit_u:
            o_ref[...] = h
            u_ref[...] = _rms(h, g2_ref[...]).astype(BF16)
        else:
            o_ref[...] = _rms(h, g2_ref[...])


def _ffn(x, g, w_gu, w_down, g2, *, emit_u, tm, tf=512):
    T = x.shape[0]
    nf = D_FF // tf
    grid = (T // tm, nf)
    in_specs = [
        pl.BlockSpec((tm, D_MODEL), lambda i, f: (i, 0)),
        pl.BlockSpec((1, D_MODEL), lambda i, f: (0, 0)),
        pl.BlockSpec((D_MODEL, tf), lambda i, f: (0, f)),
        pl.BlockSpec((D_MODEL, tf), lambda i, f: (0, nf + f)),
        pl.BlockSpec((tf, D_MODEL), lambda i, f: (f, 0)),
        pl.BlockSpec((1, D_MODEL), lambda i, f: (0, 0)),
    ]
    row_spec = pl.BlockSpec((tm, D_MODEL), lambda i, f: (i, 0))
    if emit_u:
        out_shape = (jax.ShapeDtypeStruct((T, D_MODEL), F32), jax.ShapeDtypeStruct((T, D_MODEL), BF16))
        out_specs = (row_spec, row_spec)
    else:
        out_shape = jax.ShapeDtypeStruct((T, D_MODEL), F32)
        out_specs = row_spec
    return pl.pallas_call(
        functools.partial(_ffn_kernel, nf=nf, emit_u=emit_u),
        grid=grid, in_specs=in_specs, out_specs=out_specs, out_shape=out_shape,
        scratch_shapes=[pltpu.VMEM((tm, D_MODEL), BF16)],
        compiler_params=_params(("parallel", "arbitrary"),
                                V7X_VMEM_LIMIT_BYTES if tm <= FFN1_ROWS else V7X_VMEM_LIMIT_MAX_BYTES),
        name="ffn_u" if emit_u else "ffn_final",
    )(x, g, w_gu, w_gu, w_down, g2)


def _mm_nn_kernel(x_ref, w_ref, o_ref):
    o_ref[...] = jnp.dot(x_ref[...], w_ref[...], preferred_element_type=F32).astype(o_ref.dtype)


def _mm_nn(x, w, out_dtype, *, tm=1024, tn=1024, name):
    T, K = x.shape
    N = w.shape[1]
    return pl.pallas_call(
        _mm_nn_kernel,
        grid=(T // tm, N // tn),
        in_specs=[pl.BlockSpec((tm, K), lambda i, j: (i, 0)),
                  pl.BlockSpec((K, tn), lambda i, j: (0, j))],
        out_specs=pl.BlockSpec((tm, tn), lambda i, j: (i, j)),
        out_shape=jax.ShapeDtypeStruct((T, N), out_dtype),
        compiler_params=_params(("parallel", "arbitrary")),
        name=name,
    )(x, w)


def _mm_nt_kernel(wt_ref, x_ref, o_ref):
    o_ref[...] = lax.dot_general(wt_ref[...], x_ref[...], (((1,), (1,)), ((), ())),
                                 preferred_element_type=F32).astype(o_ref.dtype)


def _mm_nt(wt, x, out_dtype, *, tm=1024, name):
    N, K = wt.shape
    T = x.shape[0]
    return pl.pallas_call(
        _mm_nt_kernel,
        grid=(T // tm,),
        in_specs=[pl.BlockSpec((N, K), lambda i: (0, 0)),
                  pl.BlockSpec((tm, K), lambda i: (i, 0))],
        out_specs=pl.BlockSpec((N, tm), lambda i: (0, i)),
        out_shape=jax.ShapeDtypeStruct((N, T), out_dtype),
        compiler_params=_params(("parallel",)),
        name=name,
    )(wt, x)


def _attn_kernel(slope_ref, lq1_ref, lk1_ref, lq2_ref, lk2_ref, q_ref, k_ref, vt_ref, g_ref, o_ref,
                 ka_ref, vta_ref, corr_ref, qa_ref, s0_ref, s1_ref, out0_ref, *, seq, t):
    h = pl.program_id(1)
    n = seq // t
    c2 = slope_ref[h] * LOG2E
    s_refs = (s0_ref, s1_ref)

    def build_key_side():
        j = lax.broadcasted_iota(jnp.int32, (seq, HEAD_W), 0)
        lane = lax.broadcasted_iota(jnp.int32, (seq, HEAD_W), 1)
        sub = lane & (BIAS_GROUP - 1)
        j_lo = j & (ALIBI_SPLIT - 1)
        pos = jnp.where(lane < 3 * BIAS_GROUP,
                        jnp.where(sub == 0, j_lo, jnp.where(sub == 1, j - j_lo, jnp.where(sub == 2, 1, 0))), 0)
        pos = pos.astype(F32).astype(BF16)
        kf = k_ref[...].astype(F32)
        for mp in range(2):
            km = kf[:, mp * QK_DIM:(mp + 1) * QK_DIM]
            ka_ref[mp, :, :HEAD_W] = jnp.concatenate([km, km], axis=1).astype(BF16)
            ka_ref[mp, :, HEAD_W:] = pos
        vta_ref[:V_DIM, :] = vt_ref[...]
        vta_ref[V_DIM:, :] = jnp.ones((ONES_ROWS, seq), BF16)
        d = lax.broadcasted_iota(jnp.int32, (t, t), 0) - lax.broadcasted_iota(jnp.int32, (t, t), 1)
        corr_ref[:t, :] = (-2.0 * c2) * jnp.maximum(d, 0).astype(F32)
        corr_ref[t:, :] = jnp.zeros((t, t), F32)

    def build_query_side(i):
        slot = i & 1
        rows = pl.ds(pl.multiple_of(i * t, t), t)
        qa = q_ref[rows, :].astype(F32).T * (QK_DIM ** -0.5 * LOG2E)
        q_hi = qa.astype(BF16)
        q_lo = (qa - q_hi.astype(F32)).astype(BF16)
        row = lax.broadcasted_iota(jnp.int32, (4 * BIAS_GROUP, t), 0)
        col = lax.broadcasted_iota(jnp.int32, (4 * BIAS_GROUP, t), 1)
        sub = row & (BIAS_GROUP - 1)
        q0c = c2 * (i * t).astype(F32)

        def bias_rows(base):
            hi = base.astype(BF16).astype(F32)
            mid = (base - hi).astype(BF16).astype(F32)
            lo = base - hi - mid
            b = jnp.where(row < BIAS_GROUP, hi, jnp.where(row < 2 * BIAS_GROUP, mid,
                                                          jnp.where(row < 3 * BIAS_GROUP, lo, 0.0)))
            return jnp.concatenate([b, jnp.zeros((HEAD_W - 4 * BIAS_GROUP, t), F32)], axis=0).astype(BF16)

        biases = (bias_rows(jnp.where(sub < 2, c2, jnp.where(sub == 2, -q0c, 0.0))),
                  bias_rows(jnp.where(sub < 2, -c2,
                                      jnp.where(sub == 2, q0c + (2.0 * c2) * col.astype(F32), 0.0))))
        for mp in range(2):
            dims = slice(mp * QK_DIM, (mp + 1) * QK_DIM)
            for side in range(2):
                qa_ref[slot, mp, side, :QK_DIM, :] = q_hi[dims]
                qa_ref[slot, mp, side, QK_DIM:HEAD_W, :] = q_lo[dims]
                qa_ref[slot, mp, side, HEAD_W:, :] = biases[side]

    def score_step(mp, i, ci, m):
        r0 = pl.multiple_of(ci * t, t)
        q_aug = qa_ref[i & 1, mp, (ci > i).astype(jnp.int32)]
        corr = corr_ref[pl.ds(pl.multiple_of(jnp.where(ci == i, 0, t), t), t), :]
        s = jnp.dot(ka_ref[mp, pl.ds(r0, t), :], q_aug, preferred_element_type=F32) + corr
        s_refs[mp][pl.ds(r0, t), :] = s
        return jnp.maximum(m, jnp.max(s, axis=0, keepdims=True))

    def value_step(mp, ci, m, acc):
        r0 = pl.multiple_of(ci * t, t)
        e = jnp.exp2(s_refs[mp][pl.ds(r0, t), :] - m).astype(BF16)
        return acc + jnp.dot(vta_ref[:, pl.ds(r0, t)], e, preferred_element_type=F32)

    unroll = min(n, 4)
    m_init = jnp.full((1, t), -1e30, F32)
    acc_init = jnp.zeros((V_DIM + ONES_ROWS, t), F32)

    def scores_and_values(mp_s, i_s, mp_v, m_v):
        return lax.fori_loop(
            0, n, lambda ci, c: (score_step(mp_s, i_s, ci, c[0]), value_step(mp_v, ci, m_v, c[1])),
            (m_init, acc_init), unroll=unroll)

    def normalised(acc):
        return acc[:V_DIM] * (1.0 / acc[V_DIM:V_DIM + 1])

    lam = (jnp.exp(jnp.sum(lq1_ref[...] * lk1_ref[...], axis=-1, keepdims=True))
           - jnp.exp(jnp.sum(lq2_ref[...] * lk2_ref[...], axis=-1, keepdims=True)) + LAMBDA_INIT)

    def write_tile(i, out0, acc1):
        o = (out0 - lam * normalised(acc1)).T
        o_ref[pl.ds(pl.multiple_of(i * t, t), t), :] = (
            _rms(o, g_ref[...]) * (1.0 - LAMBDA_INIT)).astype(o_ref.dtype)

    first, last = jnp.int32(0), jnp.int32(n - 1)
    build_key_side()
    build_query_side(first)
    m0 = lax.fori_loop(0, n, functools.partial(score_step, 0, first), m_init, unroll=unroll)

    def tile_step(i, m0):
        m1, acc0 = scores_and_values(1, i, 0, m0)
        out0_ref[...] = normalised(acc0)
        build_query_side(i + 1)
        m0_next, acc1 = scores_and_values(0, i + 1, 1, m1)
        write_tile(i, out0_ref[...], acc1)
        return m0_next

    m0 = lax.fori_loop(0, n - 1, tile_step, m0)
    m1, acc0 = scores_and_values(1, last, 0, m0)
    out0_ref[...] = normalised(acc0)
    acc1 = lax.fori_loop(0, n, lambda ci, acc: value_step(1, ci, m1, acc), acc_init, unroll=unroll)
    write_tile(last, out0_ref[...], acc1)


def _attn_tile(seq):
    for t in (512, 256, 128):
        if seq % t == 0 and seq * 2 * t * 4 <= V7X_VMEM_BYTES // 4:
            return t
    raise ValueError(f"no attention tile for sequence length {seq}")


def _attention(qk, vt, slopes, lq1, lk1, lq2, lk2, subln_g, *, batch, seq, t):
    vec = pl.BlockSpec((1, QK_DIM), lambda b, h: (0, 0))
    return pl.pallas_call(
        functools.partial(_attn_kernel, seq=seq, t=t),
        grid=(batch, N_HEADS),
        in_specs=[
            pl.BlockSpec(memory_space=pltpu.SMEM),
            vec, vec, vec, vec,
            pl.BlockSpec((seq, HEAD_W), lambda b, h: (b, h)),
            pl.BlockSpec((seq, HEAD_W), lambda b, h: (b, N_HEADS + h)),
            pl.BlockSpec((V_DIM, seq), lambda b, h: (h, b)),
            pl.BlockSpec((1, V_DIM), lambda b, h: (0, 0)),
        ],
        out_specs=pl.BlockSpec((seq, V_DIM), lambda b, h: (b, h)),
        out_shape=jax.ShapeDtypeStruct((batch * seq, D_ATTN), BF16),
        scratch_shapes=[
            pltpu.VMEM((2, seq, 2 * HEAD_W), BF16),
            pltpu.VMEM((V_DIM + ONES_ROWS, seq), BF16),
            pltpu.VMEM((2 * t, t), F32),
            pltpu.VMEM((2, 2, 2, 2 * HEAD_W, t), BF16),
            pltpu.VMEM((seq, t), F32),
            pltpu.VMEM((seq, t), F32),
            pltpu.VMEM((V_DIM, t), F32),
        ],
        compiler_params=_params(("parallel", "parallel")),
        name="diff_attn",
    )(slopes, lq1, lk1, lq2, lk2, qk, qk, vt, subln_g)


def _pool_kernel(prev_ref, x_ref, next_ref, w_ref, s_ref, o_ref, *, seq, tm):
    i = pl.program_id(1)
    n = tm + 2 * POOL_HALO
    t = i * tm + lax.broadcasted_iota(jnp.int32, (tm, 1), 0)
    first = i == 0
    last = i == pl.num_programs(1) - 1
    for g, w in enumerate(POOL_WINDOWS):
        cols = slice(g * POOL_GROUP_DIM, (g + 1) * POOL_GROUP_DIM)
        x = x_ref[:, cols]
        prev = jnp.where(first, 0.0, prev_ref[:, cols])
        nxt = jnp.where(last, 0.0, next_ref[:, cols])
        a = jnp.concatenate([prev, x, nxt], axis=0)
        span = 1
        while span < w:
            a = a + pltpu.roll(a, n - span, axis=0)
            span *= 2
        off = POOL_HALO - w // 2
        if off:
            a = pltpu.roll(a, n - off, axis=0)
        wsum = a[:tm]
        lo = jnp.maximum(t - w // 2, 0)
        hi = jnp.minimum(t + w // 2 - 1, seq - 1)
        pooled = wsum / (hi - lo + 1).astype(F32) - x
        y = jnp.dot(pooled.astype(BF16), w_ref[g], preferred_element_type=F32)
        o_ref[:, cols] = (y * s_ref[:, cols]).astype(o_ref.dtype)


def _pool(rest, w_grp, scale, *, batch, seq, col_block, tm=512):
    nt = seq // tm
    hb = tm // POOL_HALO
    n_hblk = seq // POOL_HALO
    return pl.pallas_call(
        functools.partial(_pool_kernel, seq=seq, tm=tm),
        grid=(batch, nt),
        in_specs=[
            pl.BlockSpec((POOL_HALO, D_POOL),
                         lambda b, i: (b * n_hblk + jnp.maximum(i * hb - 1, 0), col_block)),
            pl.BlockSpec((tm, D_POOL), lambda b, i: (b * nt + i, col_block)),
            pl.BlockSpec((POOL_HALO, D_POOL),
                         lambda b, i: (b * n_hblk + jnp.minimum((i + 1) * hb, n_hblk - 1), col_block)),
            pl.BlockSpec((len(POOL_WINDOWS), POOL_GROUP_DIM, POOL_GROUP_DIM), lambda b, i: (0, 0, 0)),
            pl.BlockSpec((1, D_POOL), lambda b, i: (0, 0)),
        ],
        out_specs=pl.BlockSpec((tm, D_POOL), lambda b, i: (b * nt + i, 0)),
        out_shape=jax.ShapeDtypeStruct((batch * seq, D_POOL), BF16),
        compiler_params=_params(("parallel", "arbitrary")),
        name="pool_mixer",
    )(rest, rest, rest, w_grp, scale)


def _mix_kernel(at_ref, y_ref, ga_ref, gp_ref, h_ref, wa_ref, wp_ref, wo_ref, o_ref):
    a = jnp.dot(at_ref[...], wa_ref[...], preferred_element_type=F32)
    p = jnp.dot(y_ref[...], wp_ref[...], preferred_element_type=F32)
    merged = jax.nn.sigmoid(ga_ref[...]) * a + jax.nn.sigmoid(gp_ref[...]) * p
    o_ref[...] = h_ref[...] + jnp.dot(merged.astype(BF16), wo_ref[...], preferred_element_type=F32)


def _mix(attn, y, rest, h, w_a, w_p, w_o, *, tm=256):
    T = h.shape[0]
    const = lambda i: (0, 0)
    return pl.pallas_call(
        _mix_kernel,
        grid=(T // tm,),
        in_specs=[
            pl.BlockSpec((tm, D_ATTN), lambda i: (i, 0)),
            pl.BlockSpec((tm, D_POOL), lambda i: (i, 0)),
            pl.BlockSpec((tm, D_MODEL), lambda i: (i, 0)),
            pl.BlockSpec((tm, D_MODEL), lambda i: (i, 1)),
            pl.BlockSpec((tm, D_MODEL), lambda i: (i, 0)),
            pl.BlockSpec((D_ATTN, D_MODEL), const, pipeline_mode=pl.Buffered(1)),
            pl.BlockSpec((D_POOL, D_MODEL), const, pipeline_mode=pl.Buffered(1)),
            pl.BlockSpec((D_MODEL, D_MODEL), const, pipeline_mode=pl.Buffered(1)),
        ],
        out_specs=pl.BlockSpec((tm, D_MODEL), lambda i: (i, 0)),
        out_shape=jax.ShapeDtypeStruct((T, D_MODEL), F32),
        compiler_params=_params(("parallel",)),
        name="gated_mix",
    )(attn, y, rest, rest, h, w_a, w_p, w_o)


def _trunk(x, p):
    batch, seq, _ = x.shape
    xf = x.reshape(batch * seq, D_MODEL)
    h, u = _ffn(xf, p["ffn1_norm"], p["ffn1_w_gu"], p["ffn1_w_down"], p["mix_norm"], emit_u=True,
                tm=FFN1_ROWS)
    qk = _mm_nn(u, p["w_qk"], BF16, name="proj_qk")
    vt = _mm_nt(p["w_v_t"], u, BF16, name="proj_vt")
    rest = _mm_nn(u, p["w_rest"], F32, name="proj_rest")
    attn = _attention(qk, vt, p["slopes"], p["lq1"], p["lk1"], p["lq2"], p["lk2"], p["subln_g"],
                      batch=batch, seq=seq, t=_attn_tile(seq))
    y = _pool(rest, p["w_pool_grp"], p["pool_scale"], batch=batch, seq=seq,
              col_block=2 * D_MODEL // D_POOL)
    h2 = _mix(attn, y, rest, h, p["w_attn_proj"], p["w_pool_proj"], p["w_out"])
    out = _ffn(h2, p["ffn2_norm"], p["ffn2_w_gu"], p["ffn2_w_down"], p["final_norm"], emit_u=False,
               tm=FFN2_ROWS)
    return out.reshape(batch, seq, D_MODEL)


def kernel(x_prompt, x_sample, ffn1_norm, ffn1_w_gu, ffn1_w_down, mix_norm, w_in, lambda_q1, lambda_k1, lambda_q2, lambda_k2, attn_subln_g, w_attn_proj, w_pool_grp, pool_scale, w_pool_proj, w_out, ffn2_norm, ffn2_w_gu, ffn2_w_down, final_norm):
    l = 0
    w = w_in[l]
    c_v, c_p, c_ga = 2 * D_Q, 2 * D_Q + D_ATTN, 2 * D_Q + D_ATTN + D_POOL
    p = {
        "ffn1_norm": ffn1_norm[l][None], "mix_norm": mix_norm[l][None],
        "ffn2_norm": ffn2_norm[l][None], "final_norm": final_norm[None],
        "ffn1_w_gu": ffn1_w_gu[l].astype(BF16), "ffn1_w_down": ffn1_w_down[l].astype(BF16),
        "ffn2_w_gu": ffn2_w_gu[l].astype(BF16), "ffn2_w_down": ffn2_w_down[l].astype(BF16),
        "w_qk": w[:, :c_v].astype(BF16),
        "w_v_t": w[:, c_v:c_p].T.astype(BF16),
        "w_rest": jnp.concatenate([w[:, c_ga:], w[:, c_p:c_ga]], axis=1).astype(BF16),
        "slopes": jnp.asarray(2.0 ** (-8.0 * np.arange(1, N_HEADS + 1) / N_HEADS), dtype=F32),
        "lq1": lambda_q1[l][None], "lk1": lambda_k1[l][None],
        "lq2": lambda_q2[l][None], "lk2": lambda_k2[l][None],
        "subln_g": attn_subln_g[l][None],
        "w_attn_proj": w_attn_proj[l].astype(BF16),
        "w_pool_grp": w_pool_grp[l].astype(BF16),
        "pool_scale": pool_scale[l][None],
        "w_pool_proj": w_pool_proj[l].astype(BF16),
        "w_out": w_out[l].astype(BF16),
    }
    return (_trunk(x_prompt, p), _trunk(x_sample, p))
```

```python
import functools

import jax
import jax.numpy as jnp
import numpy as np
from jax import lax
from jax.experimental import pallas as pl
from jax.experimental.pallas import tpu as pltpu

D_MODEL = 2048
N_HEADS = 8
QK_DIM = 64
V_DIM = 2 * QK_DIM
HEAD_W = 2 * QK_DIM
D_ATTN = N_HEADS * V_DIM
D_Q = N_HEADS * HEAD_W
ALIBI_SPLIT = 64
BIAS_GROUP = 8
ONES_ROWS = 16
LOG2E = 1.4426950408889634
POOL_WINDOWS = (2, 4, 8, 16)
POOL_GROUP_DIM = 256
D_POOL = len(POOL_WINDOWS) * POOL_GROUP_DIM
POOL_HALO = 8
D_FF = 5632
EPS = 1e-6
LAMBDA_INIT = 0.8 - 0.6 * float(np.exp(-0.3 * 0))

V7X_VMEM_BYTES = 64 * 1024 * 1024
V7X_VMEM_LIMIT_BYTES = V7X_VMEM_BYTES - 8 * 1024 * 1024
V7X_VMEM_LIMIT_MAX_BYTES = V7X_VMEM_BYTES - 512 * 1024
FFN1_ROWS = 512
FFN2_ROWS = 1024

F32 = jnp.float32
BF16 = jnp.bfloat16


def _params(semantics, vmem_limit_bytes=V7X_VMEM_LIMIT_BYTES):
    return pltpu.CompilerParams(dimension_semantics=semantics, vmem_limit_bytes=vmem_limit_bytes)


def _rms(xf, g):
    return xf * lax.rsqrt(jnp.mean(xf * xf, axis=-1, keepdims=True) + EPS) * g


def _ffn_kernel(x_ref, g_ref, wg_ref, wu_ref, wd_ref, g2_ref, *refs, nf, emit_u):
    if emit_u:
        o_ref, u_ref, xn_ref = refs
    else:
        o_ref, xn_ref = refs
    f = pl.program_id(1)

    @pl.when(f == 0)
    def _():
        xn_ref[...] = _rms(x_ref[...], g_ref[...]).astype(BF16)
        o_ref[...] = jnp.zeros_like(o_ref)

    xn = xn_ref[...]
    gate = jnp.dot(xn, wg_ref[...], preferred_element_type=F32)
    up = jnp.dot(xn, wu_ref[...], preferred_element_type=F32)
    act = (gate * jax.nn.sigmoid(gate) * up).astype(BF16)
    o_ref[...] += jnp.dot(act, wd_ref[...], preferred_element_type=F32)

    @pl.when(f == nf - 1)
    def _():
        h = x_ref[...] + 0.5 * o_ref[...]
        if emit_u:
            o_ref[...] = h
            u_ref[...] = _rms(h, g2_ref[...]).astype(BF16)
        else:
            o_ref[...] = _rms(h, g2_ref[...])


def _ffn(x, g, w_gu, w_down, g2, *, emit_u, tm, tf=512):
    T = x.shape[0]
    nf = D_FF // tf
    grid = (T // tm, nf)
    in_specs = [
        pl.BlockSpec((tm, D_MODEL), lambda i, f: (i, 0)),
        pl.BlockSpec((1, D_MODEL), lambda i, f: (0, 0)),
        pl.BlockSpec((D_MODEL, tf), lambda i, f: (0, f)),
        pl.BlockSpec((D_MODEL, tf), lambda i, f: (0, nf + f)),
        pl.BlockSpec((tf, D_MODEL), lambda i, f: (f, 0)),
        pl.BlockSpec((1, D_MODEL), lambda i, f: (0, 0)),
    ]
    row_spec = pl.BlockSpec((tm, D_MODEL), lambda i, f: (i, 0))
    if emit_u:
        out_shape = (jax.ShapeDtypeStruct((T, D_MODEL), F32), jax.ShapeDtypeStruct((T, D_MODEL), BF16))
        out_specs = (row_spec, row_spec)
    else:
        out_shape = jax.ShapeDtypeStruct((T, D_MODEL), F32)
        out_specs = row_spec
    return pl.pallas_call(
        functools.partial(_ffn_kernel, nf=nf, emit_u=emit_u),
        grid=grid, in_specs=in_specs, out_specs=out_specs, out_shape=out_shape,
        scratch_shapes=[pltpu.VMEM((tm, D_MODEL), BF16)],
        compiler_params=_params(("parallel", "arbitrary"),
                                V7X_VMEM_LIMIT_BYTES if tm <= FFN1_ROWS else V7X_VMEM_LIMIT_MAX_BYTES),
        name="ffn_u" if emit_u else "ffn_final",
    )(x, g, w_gu, w_gu, w_down, g2)


def _mm_nn_kernel(x_ref, w_ref, o_ref):
    o_ref[...] = jnp.dot(x_ref[...], w_ref[...], preferred_element_type=F32).astype(o_ref.dtype)


def _mm_nn(x, w, out_dtype, *, tm=1024, tn=1024, name):
    T, K = x.shape
    N = w.shape[1]
    return pl.pallas_call(
        _mm_nn_kernel,
        grid=(T // tm, N // tn),
        in_specs=[pl.BlockSpec((tm, K), lambda i, j: (i, 0)),
                  pl.BlockSpec((K, tn), lambda i, j: (0, j))],
        out_specs=pl.BlockSpec((tm, tn), lambda i, j: (i, j)),
        out_shape=jax.ShapeDtypeStruct((T, N), out_dtype),
        compiler_params=_params(("parallel", "arbitrary")),
        name=name,
    )(x, w)


def _mm_nt_kernel(wt_ref, x_ref, o_ref):
    o_ref[...] = lax.dot_general(wt_ref[...], x_ref[...], (((1,), (1,)), ((), ())),
                                 preferred_element_type=F32).astype(o_ref.dtype)


def _mm_nt(wt, x, out_dtype, *, tm=1024, name):
    N, K = wt.shape
    T = x.shape[0]
    return pl.pallas_call(
        _mm_nt_kernel,
        grid=(T // tm,),
        in_specs=[pl.BlockSpec((N, K), lambda i: (0, 0)),
                  pl.BlockSpec((tm, K), lambda i: (i, 0))],
        out_specs=pl.BlockSpec((N, tm), lambda i: (0, i)),
        out_shape=jax.ShapeDtypeStruct((N, T), out_dtype),
        compiler_params=_params(("parallel",)),
        name=name,
    )(wt, x)


def _attn_kernel(slope_ref, lq1_ref, lk1_ref, lq2_ref, lk2_ref, q_ref, k_ref, vt_ref, g_ref, o_ref,
                 ka_ref, vta_ref, relu_ref, corr_ref, qa_ref, s0_ref, s1_ref, out0_ref, *, seq, t):
    h = pl.program_id(1)
    n = seq // t
    c2 = slope_ref[h] * LOG2E
    s_refs = (s0_ref, s1_ref)

    @pl.when((pl.program_id(0) == 0) & (h == 0))
    def _():
        j = lax.broadcasted_iota(jnp.int32, (seq, HEAD_W), 0)
        lane = lax.broadcasted_iota(jnp.int32, (seq, HEAD_W), 1)
        sub = lane & (BIAS_GROUP - 1)
        j_lo = j & (ALIBI_SPLIT - 1)
        pos = jnp.where(lane < 3 * BIAS_GROUP,
                        jnp.where(sub == 0, j_lo, jnp.where(sub == 1, j - j_lo, jnp.where(sub == 2, 1, 0))), 0)
        pos = pos.astype(F32).astype(BF16)
        for mp in range(2):
            ka_ref[mp, :, HEAD_W:] = pos
        vta_ref[V_DIM:, :] = jnp.ones((ONES_ROWS, seq), BF16)
        d = lax.broadcasted_iota(jnp.int32, (t, t), 0) - lax.broadcasted_iota(jnp.int32, (t, t), 1)
        relu_ref[...] = jnp.maximum(d, 0).astype(F32)
        corr_ref[t:, :] = jnp.zeros((t, t), F32)

    def build_key_side():
        kf = k_ref[...].astype(F32)
        for mp in range(2):
            km = kf[:, mp * QK_DIM:(mp + 1) * QK_DIM]
            ka_ref[mp, :, :HEAD_W] = jnp.concatenate([km, km], axis=1).astype(BF16)
        vta_ref[:V_DIM, :] = vt_ref[...]
        corr_ref[:t, :] = (-2.0 * c2) * relu_ref[...]

    def build_query_side(i):
        slot = i & 1
        rows = pl.ds(pl.multiple_of(i * t, t), t)
        qa = q_ref[rows, :].astype(F32).T * (QK_DIM ** -0.5 * LOG2E)
        q_hi = qa.astype(BF16)
        q_lo = (qa - q_hi.astype(F32)).astype(BF16)
        row = lax.broadcasted_iota(jnp.int32, (4 * BIAS_GROUP, t), 0)
        col = lax.broadcasted_iota(jnp.int32, (4 * BIAS_GROUP, t), 1)
        sub = row & (BIAS_GROUP - 1)
        q0c = c2 * (i * t).astype(F32)

        def bias_rows(base):
            hi = base.astype(BF16).astype(F32)
            mid = (base - hi).astype(BF16).astype(F32)
            lo = base - hi - mid
            b = jnp.where(row < BIAS_GROUP, hi, jnp.where(row < 2 * BIAS_GROUP, mid,
                                                          jnp.where(row < 3 * BIAS_GROUP, lo, 0.0)))
            return jnp.concatenate([b, jnp.zeros((HEAD_W - 4 * BIAS_GROUP, t), F32)], axis=0).astype(BF16)

        biases = (bias_rows(jnp.where(sub < 2, c2, jnp.where(sub == 2, -q0c, 0.0))),
                  bias_rows(jnp.where(sub < 2, -c2,
                                      jnp.where(sub == 2, q0c + (2.0 * c2) * col.astype(F32), 0.0))))
        for mp in range(2):
            dims = slice(mp * QK_DIM, (mp + 1) * QK_DIM)
            for side in range(2):
                qa_ref[slot, mp, side, :QK_DIM, :] = q_hi[dims]
                qa_ref[slot, mp, side, QK_DIM:HEAD_W, :] = q_lo[dims]
                qa_ref[slot, mp, side, HEAD_W:, :] = biases[side]

    def score_step(mp, i, ci, m):
        r0 = pl.multiple_of(ci * t, t)
        q_aug = qa_ref[i & 1, mp, (ci > i).astype(jnp.int32)]
        corr = corr_ref[pl.ds(pl.multiple_of(jnp.where(ci == i, 0, t), t), t), :]
        s = jnp.dot(ka_ref[mp, pl.ds(r0, t), :], q_aug, preferred_element_type=F32) + corr
        s_refs[mp][pl.ds(r0, t), :] = s
        return jnp.maximum(m, jnp.max(s, axis=0, keepdims=True))

    def value_step(mp, ci, m, acc):
        r0 = pl.multiple_of(ci * t, t)
        e = jnp.exp2(s_refs[mp][pl.ds(r0, t), :] - m).astype(BF16)
        return acc + jnp.dot(vta_ref[:, pl.ds(r0, t)], e, preferred_element_type=F32)

    unroll = min(n, 8)
    m_init = jnp.full((1, t), -1e30, F32)
    acc_init = jnp.zeros((V_DIM + ONES_ROWS, t), F32)

    def scores_and_values(mp_s, i_s, mp_v, m_v):
        return lax.fori_loop(
            0, n, lambda ci, c: (score_step(mp_s, i_s, ci, c[0]), value_step(mp_v, ci, m_v, c[1])),
            (m_init, acc_init), unroll=unroll)

    def normalised(acc):
        return acc[:V_DIM] * (1.0 / acc[V_DIM:V_DIM + 1])

    lam = (jnp.exp(jnp.sum(lq1_ref[...] * lk1_ref[...], axis=-1, keepdims=True))
           - jnp.exp(jnp.sum(lq2_ref[...] * lk2_ref[...], axis=-1, keepdims=True)) + LAMBDA_INIT)

    def write_tile(i, out0, acc1):
        o = (out0 - lam * normalised(acc1)).T
        o_ref[pl.ds(pl.multiple_of(i * t, t), t), :] = (
            _rms(o, g_ref[...]) * (1.0 - LAMBDA_INIT)).astype(o_ref.dtype)

    first, last = jnp.int32(0), jnp.int32(n - 1)
    build_key_side()
    build_query_side(first)
    m0 = lax.fori_loop(0, n, functools.partial(score_step, 0, first), m_init, unroll=unroll)

    def tile_step(i, m0):
        m1, acc0 = scores_and_values(1, i, 0, m0)
        out0_ref[...] = normalised(acc0)
        build_query_side(i + 1)
        m0_next, acc1 = scores_and_values(0, i + 1, 1, m1)
        write_tile(i, out0_ref[...], acc1)
        return m0_next

    m0 = lax.fori_loop(0, n - 1, tile_step, m0)
    m1, acc0 = scores_and_values(1, last, 0, m0)
    out0_ref[...] = normalised(acc0)
    acc1 = lax.fori_loop(0, n, lambda ci, acc: value_step(1, ci, m1, acc), acc_init, unroll=unroll)
    write_tile(last, out0_ref[...], acc1)


def _attn_tile(seq):
    for t in (512, 256, 128):
        if seq % t == 0 and seq * 2 * t * 4 <= V7X_VMEM_BYTES // 4:
            return t
    raise ValueError(f"no attention tile for sequence length {seq}")


def _attention(qk, vt, slopes, lq1, lk1, lq2, lk2, subln_g, *, batch, seq, t):
    vec = pl.BlockSpec((1, QK_DIM), lambda b, h: (0, 0))
    return pl.pallas_call(
        functools.partial(_attn_kernel, seq=seq, t=t),
        grid=(batch, N_HEADS),
        in_specs=[
            pl.BlockSpec(memory_space=pltpu.SMEM),
            vec, vec, vec, vec,
            pl.BlockSpec((seq, HEAD_W), lambda b, h: (b, h)),
            pl.BlockSpec((seq, HEAD_W), lambda b, h: (b, N_HEADS + h)),
            pl.BlockSpec((V_DIM, seq), lambda b, h: (h, b)),
            pl.BlockSpec((1, V_DIM), lambda b, h: (0, 0)),
        ],
        out_specs=pl.BlockSpec((seq, V_DIM), lambda b, h: (b, h)),
        out_shape=jax.ShapeDtypeStruct((batch * seq, D_ATTN), BF16),
        scratch_shapes=[
            pltpu.VMEM((2, seq, 2 * HEAD_W), BF16),
            pltpu.VMEM((V_DIM + ONES_ROWS, seq), BF16),
            pltpu.VMEM((t, t), F32),
            pltpu.VMEM((2 * t, t), F32),
            pltpu.VMEM((2, 2, 2, 2 * HEAD_W, t), BF16),
            pltpu.VMEM((seq, t), F32),
            pltpu.VMEM((seq, t), F32),
            pltpu.VMEM((V_DIM, t), F32),
        ],
        compiler_params=_params(("arbitrary", "arbitrary")),
        name="diff_attn",
    )(slopes, lq1, lk1, lq2, lk2, qk, qk, vt, subln_g)


def _pool_kernel(prev_ref, x_ref, next_ref, w_ref, s_ref, o_ref, *, seq, tm):
    i = pl.program_id(1)
    n = tm + 2 * POOL_HALO
    t = i * tm + lax.broadcasted_iota(jnp.int32, (tm, 1), 0)
    first = i == 0
    last = i == pl.num_programs(1) - 1
    for g, w in enumerate(POOL_WINDOWS):
        cols = slice(g * POOL_GROUP_DIM, (g + 1) * POOL_GROUP_DIM)
        x = x_ref[:, cols]
        prev = jnp.where(first, 0.0, prev_ref[:, cols])
        nxt = jnp.where(last, 0.0, next_ref[:, cols])
        a = jnp.concatenate([prev, x, nxt], axis=0)
        span = 1
        while span < w:
            a = a + pltpu.roll(a, n - span, axis=0)
            span *= 2
        off = POOL_HALO - w // 2
        if off:
            a = pltpu.roll(a, n - off, axis=0)
        wsum = a[:tm]
        lo = jnp.maximum(t - w // 2, 0)
        hi = jnp.minimum(t + w // 2 - 1, seq - 1)
        pooled = wsum / (hi - lo + 1).astype(F32) - x
        y = jnp.dot(pooled.astype(BF16), w_ref[g], preferred_element_type=F32)
        o_ref[:, cols] = (y * s_ref[:, cols]).astype(o_ref.dtype)


def _pool(rest, w_grp, scale, *, batch, seq, col_block, tm=512):
    nt = seq // tm
    hb = tm // POOL_HALO
    n_hblk = seq // POOL_HALO
    return pl.pallas_call(
        functools.partial(_pool_kernel, seq=seq, tm=tm),
        grid=(batch, nt),
        in_specs=[
            pl.BlockSpec((POOL_HALO, D_POOL),
                         lambda b, i: (b * n_hblk + jnp.maximum(i * hb - 1, 0), col_block)),
            pl.BlockSpec((tm, D_POOL), lambda b, i: (b * nt + i, col_block)),
            pl.BlockSpec((POOL_HALO, D_POOL),
                         lambda b, i: (b * n_hblk + jnp.minimum((i + 1) * hb, n_hblk - 1), col_block)),
            pl.BlockSpec((len(POOL_WINDOWS), POOL_GROUP_DIM, POOL_GROUP_DIM), lambda b, i: (0, 0, 0)),
            pl.BlockSpec((1, D_POOL), lambda b, i: (0, 0)),
        ],
        out_specs=pl.BlockSpec((tm, D_POOL), lambda b, i: (b * nt + i, 0)),
        out_shape=jax.ShapeDtypeStruct((batch * seq, D_POOL), BF16),
        compiler_params=_params(("parallel", "arbitrary")),
        name="pool_mixer",
    )(rest, rest, rest, w_grp, scale)


def _mix_kernel(at_ref, y_ref, ga_ref, gp_ref, h_ref, wa_ref, wp_ref, wo_ref, o_ref):
    a = jnp.dot(at_ref[...], wa_ref[...], preferred_element_type=F32)
    p = jnp.dot(y_ref[...], wp_ref[...], preferred_element_type=F32)
    merged = jax.nn.sigmoid(ga_ref[...]) * a + jax.nn.sigmoid(gp_ref[...]) * p
    o_ref[...] = h_ref[...] + jnp.dot(merged.astype(BF16), wo_ref[...], preferred_element_type=F32)


def _mix(attn, y, rest, h, w_a, w_p, w_o, *, tm=256):
    T = h.shape[0]
    const = lambda i: (0, 0)
    return pl.pallas_call(
        _mix_kernel,
        grid=(T // tm,),
        in_specs=[
            pl.BlockSpec((tm, D_ATTN), lambda i: (i, 0)),
            pl.BlockSpec((tm, D_POOL), lambda i: (i, 0)),
            pl.BlockSpec((tm, D_MODEL), lambda i: (i, 0)),
            pl.BlockSpec((tm, D_MODEL), lambda i: (i, 1)),
            pl.BlockSpec((tm, D_MODEL), lambda i: (i, 0)),
            pl.BlockSpec((D_ATTN, D_MODEL), const, pipeline_mode=pl.Buffered(1)),
            pl.BlockSpec((D_POOL, D_MODEL), const, pipeline_mode=pl.Buffered(1)),
            pl.BlockSpec((D_MODEL, D_MODEL), const, pipeline_mode=pl.Buffered(1)),
        ],
        out_specs=pl.BlockSpec((tm, D_MODEL), lambda i: (i, 0)),
        out_shape=jax.ShapeDtypeStruct((T, D_MODEL), F32),
        compiler_params=_params(("parallel",)),
        name="gated_mix",
    )(attn, y, rest, rest, h, w_a, w_p, w_o)


def _trunk(x, p):
    batch, seq, _ = x.shape
    xf = x.reshape(batch * seq, D_MODEL)
    h, u = _ffn(xf, p["ffn1_norm"], p["ffn1_w_gu"], p["ffn1_w_down"], p["mix_norm"], emit_u=True,
                tm=FFN1_ROWS)
    qk = _mm_nn(u, p["w_qk"], BF16, name="proj_qk")
    vt = _mm_nt(p["w_v_t"], u, BF16, name="proj_vt")
    rest = _mm_nn(u, p["w_rest"], F32, name="proj_rest")
    attn = _attention(qk, vt, p["slopes"], p["lq1"], p["lk1"], p["lq2"], p["lk2"], p["subln_g"],
                      batch=batch, seq=seq, t=_attn_tile(seq))
    y = _pool(rest, p["w_pool_grp"], p["pool_scale"], batch=batch, seq=seq,
              col_block=2 * D_MODEL // D_POOL)
    h2 = _mix(attn, y, rest, h, p["w_attn_proj"], p["w_pool_proj"], p["w_out"])
    out = _ffn(h2, p["ffn2_norm"], p["ffn2_w_gu"], p["ffn2_w_down"], p["final_norm"], emit_u=False,
               tm=FFN2_ROWS)
    return out.reshape(batch, seq, D_MODEL)


def kernel(x_prompt, x_sample, ffn1_norm, ffn1_w_gu, ffn1_w_down, mix_norm, w_in, lambda_q1, lambda_k1, lambda_q2, lambda_k2, attn_subln_g, w_attn_proj, w_pool_grp, pool_scale, w_pool_proj, w_out, ffn2_norm, ffn2_w_gu, ffn2_w_down, final_norm):
    l = 0
    w = w_in[l]
    c_v, c_p, c_ga = 2 * D_Q, 2 * D_Q + D_ATTN, 2 * D_Q + D_ATTN + D_POOL
    p = {
        "ffn1_norm": ffn1_norm[l][None], "mix_norm": mix_norm[l][None],
        "ffn2_norm": ffn2_norm[l][None], "final_norm": final_norm[None],
        "ffn1_w_gu": ffn1_w_gu[l].astype(BF16), "ffn1_w_down": ffn1_w_down[l].astype(BF16),
        "ffn2_w_gu": ffn2_w_gu[l].astype(BF16), "ffn2_w_down": ffn2_w_down[l].astype(BF16),
        "w_qk": w[:, :c_v].astype(BF16),
        "w_v_t": w[:, c_v:c_p].T.astype(BF16),
        "w_rest": jnp.concatenate([w[:, c_ga:], w[:, c_p:c_ga]], axis=1).astype(BF16),
        "slopes": jnp.asarray(2.0 ** (-8.0 * np.arange(1, N_HEADS + 1) / N_HEADS), dtype=F32),
        "lq1": lambda_q1[l][None], "lk1": lambda_k1[l][None],
        "lq2": lambda_q2[l][None], "lk2": lambda_k2[l][None],
        "subln_g": attn_subln_g[l][None],
        "w_attn_proj": w_attn_proj[l].astype(BF16),
        "w_pool_grp": w_pool_grp[l].astype(BF16),
        "pool_scale": pool_scale[l][None],
        "w_pool_proj": w_pool_proj[l].astype(BF16),
        "w_out": w_out[l].astype(BF16),
    }
    return (_trunk(x_prompt, p), _trunk(x_sample, p))
```

```python
import functools

import jax
import jax.numpy as jnp
import numpy as np
from jax import lax
from jax.experimental import pallas as pl
from jax.experimental.pallas import tpu as pltpu

D_MODEL = 2048
N_HEADS = 8
QK_DIM = 64
V_DIM = 2 * QK_DIM
HEAD_W = 2 * QK_DIM
D_ATTN = N_HEADS * V_DIM
D_Q = N_HEADS * HEAD_W
ALIBI_SPLIT = 64
BIAS_GROUP = 8
ONES_ROWS = 16
LOG2E = 1.4426950408889634
POOL_WINDOWS = (2, 4, 8, 16)
POOL_GROUP_DIM = 256
D_POOL = len(POOL_WINDOWS) * POOL_GROUP_DIM
POOL_HALO = 8
D_FF = 5632
EPS = 1e-6
LAMBDA_INIT = 0.8 - 0.6 * float(np.exp(-0.3 * 0))

V7X_VMEM_BYTES = 64 * 1024 * 1024
V7X_VMEM_LIMIT_BYTES = V7X_VMEM_BYTES - 8 * 1024 * 1024
V7X_VMEM_LIMIT_MAX_BYTES = V7X_VMEM_BYTES - 512 * 1024
FFN1_ROWS = 512
FFN2_ROWS = 1024

F32 = jnp.float32
BF16 = jnp.bfloat16


def _params(semantics, vmem_limit_bytes=V7X_VMEM_LIMIT_BYTES):
    return pltpu.CompilerParams(dimension_semantics=semantics, vmem_limit_bytes=vmem_limit_bytes)


def _rms(xf, g):
    return xf * lax.rsqrt(jnp.mean(xf * xf, axis=-1, keepdims=True) + EPS) * g


def _ffn_kernel(x_ref, g_ref, wg_ref, wu_ref, wd_ref, g2_ref, *refs, nf, emit_u):
    if emit_u:
        o_ref, u_ref, xn_ref = refs
    else:
        o_ref, xn_ref = refs
    f = pl.program_id(1)

    @pl.when(f == 0)
    def _():
        xn_ref[...] = _rms(x_ref[...], g_ref[...]).astype(BF16)
        o_ref[...] = jnp.zeros_like(o_ref)

    xn = xn_ref[...]
    gate = jnp.dot(xn, wg_ref[...], preferred_element_type=F32)
    up = jnp.dot(xn, wu_ref[...], preferred_element_type=F32)
    act = (gate * jax.nn.sigmoid(gate) * up).astype(BF16)
    o_ref[...] += jnp.dot(act, wd_ref[...], preferred_element_type=F32)

    @pl.when(f == nf - 1)
    def _():
        h = x_ref[...] + 0.5 * o_ref[...]
        if emit_u:
            o_ref[...] = h
            u_ref[...] = _rms(h, g2_ref[...]).astype(BF16)
        else:
            o_ref[...] = _rms(h, g2_ref[...])


def _ffn(x, g, w_gu, w_down, g2, *, emit_u, tm, tf=512):
    T = x.shape[0]
    nf = D_FF // tf
    grid = (T // tm, nf)
    in_specs = [
        pl.BlockSpec((tm, D_MODEL), lambda i, f: (i, 0)),
        pl.BlockSpec((1, D_MODEL), lambda i, f: (0, 0)),
        pl.BlockSpec((D_MODEL, tf), lambda i, f: (0, f)),
        pl.BlockSpec((D_MODEL, tf), lambda i, f: (0, nf + f)),
        pl.BlockSpec((tf, D_MODEL), lambda i, f: (f, 0)),
        pl.BlockSpec((1, D_MODEL), lambda i, f: (0, 0)),
    ]
    row_spec = pl.BlockSpec((tm, D_MODEL), lambda i, f: (i, 0))
    if emit_u:
        out_shape = (jax.ShapeDtypeStruct((T, D_MODEL), F32), jax.ShapeDtypeStruct((T, D_MODEL), BF16))
        out_specs = (row_spec, row_spec)
    else:
        out_shape = jax.ShapeDtypeStruct((T, D_MODEL), F32)
        out_specs = row_spec
    return pl.pallas_call(
        functools.partial(_ffn_kernel, nf=nf, emit_u=emit_u),
        grid=grid, in_specs=in_specs, out_specs=out_specs, out_shape=out_shape,
        scratch_shapes=[pltpu.VMEM((tm, D_MODEL), BF16)],
        compiler_params=_params(("parallel", "arbitrary"),
                                V7X_VMEM_LIMIT_BYTES if tm <= FFN1_ROWS else V7X_VMEM_LIMIT_MAX_BYTES),
        name="ffn_u" if emit_u else "ffn_final",
    )(x, g, w_gu, w_gu, w_down, g2)


def _mm_nn_kernel(x_ref, w_ref, o_ref):
    o_ref[...] = jnp.dot(x_ref[...], w_ref[...], preferred_element_type=F32).astype(o_ref.dtype)


def _mm_nn(x, w, out_dtype, *, tm=2048, tn=1024, name):
    T, K = x.shape
    N = w.shape[1]
    return pl.pallas_call(
        _mm_nn_kernel,
        grid=(T // tm, N // tn),
        in_specs=[pl.BlockSpec((tm, K), lambda i, j: (i, 0)),
                  pl.BlockSpec((K, tn), lambda i, j: (0, j))],
        out_specs=pl.BlockSpec((tm, tn), lambda i, j: (i, j)),
        out_shape=jax.ShapeDtypeStruct((T, N), out_dtype),
        compiler_params=_params(("parallel", "arbitrary")),
        name=name,
    )(x, w)


def _mm_nt_kernel(wt_ref, x_ref, o_ref):
    o_ref[...] = lax.dot_general(wt_ref[...], x_ref[...], (((1,), (1,)), ((), ())),
                                 preferred_element_type=F32).astype(o_ref.dtype)


def _mm_nt(wt, x, out_dtype, *, tm=2048, name):
    N, K = wt.shape
    T = x.shape[0]
    return pl.pallas_call(
        _mm_nt_kernel,
        grid=(T // tm,),
        in_specs=[pl.BlockSpec((N, K), lambda i: (0, 0)),
                  pl.BlockSpec((tm, K), lambda i: (i, 0))],
        out_specs=pl.BlockSpec((N, tm), lambda i: (0, i)),
        out_shape=jax.ShapeDtypeStruct((N, T), out_dtype),
        compiler_params=_params(("parallel",)),
        name=name,
    )(wt, x)


def _attn_kernel(slope_ref, lq1_ref, lk1_ref, lq2_ref, lk2_ref, q_ref, k_ref, vt_ref, g_ref, o_ref,
                 ka_ref, vta_ref, relu_ref, corr_ref, qa_ref, s0_ref, s1_ref, out0_ref, *, seq, t):
    h = pl.program_id(1)
    n = seq // t
    c2 = slope_ref[h] * LOG2E
    s_refs = (s0_ref, s1_ref)

    @pl.when((pl.program_id(0) == 0) & (h == 0))
    def _():
        j = lax.broadcasted_iota(jnp.int32, (seq, HEAD_W), 0)
        lane = lax.broadcasted_iota(jnp.int32, (seq, HEAD_W), 1)
        sub = lane & (BIAS_GROUP - 1)
        j_lo = j & (ALIBI_SPLIT - 1)
        pos = jnp.where(lane < 3 * BIAS_GROUP,
                        jnp.where(sub == 0, j_lo, jnp.where(sub == 1, j - j_lo, jnp.where(sub == 2, 1, 0))), 0)
        pos = pos.astype(F32).astype(BF16)
        for mp in range(2):
            ka_ref[mp, :, HEAD_W:] = pos
        vta_ref[V_DIM:, :] = jnp.ones((ONES_ROWS, seq), BF16)
        d = lax.broadcasted_iota(jnp.int32, (t, t), 0) - lax.broadcasted_iota(jnp.int32, (t, t), 1)
        relu_ref[...] = jnp.maximum(d, 0).astype(F32)
        corr_ref[t:, :] = jnp.zeros((t, t), F32)

    def build_key_side():
        kf = k_ref[...].astype(F32)
        for mp in range(2):
            km = kf[:, mp * QK_DIM:(mp + 1) * QK_DIM]
            ka_ref[mp, :, :HEAD_W] = jnp.concatenate([km, km], axis=1).astype(BF16)
        vta_ref[:V_DIM, :] = vt_ref[...]
        corr_ref[:t, :] = (-2.0 * c2) * relu_ref[...]

    def build_query_side(i):
        slot = i & 1
        rows = pl.ds(pl.multiple_of(i * t, t), t)
        qa = q_ref[rows, :].astype(F32).T * (QK_DIM ** -0.5 * LOG2E)
        q_hi = qa.astype(BF16)
        q_lo = (qa - q_hi.astype(F32)).astype(BF16)
        row = lax.broadcasted_iota(jnp.int32, (4 * BIAS_GROUP, t), 0)
        col = lax.broadcasted_iota(jnp.int32, (4 * BIAS_GROUP, t), 1)
        sub = row & (BIAS_GROUP - 1)
        q0c = c2 * (i * t).astype(F32)

        def bias_rows(base):
            hi = base.astype(BF16).astype(F32)
            mid = (base - hi).astype(BF16).astype(F32)
            lo = base - hi - mid
            b = jnp.where(row < BIAS_GROUP, hi, jnp.where(row < 2 * BIAS_GROUP, mid,
                                                          jnp.where(row < 3 * BIAS_GROUP, lo, 0.0)))
            return jnp.concatenate([b, jnp.zeros((HEAD_W - 4 * BIAS_GROUP, t), F32)], axis=0).astype(BF16)

        biases = (bias_rows(jnp.where(sub < 2, c2, jnp.where(sub == 2, -q0c, 0.0))),
                  bias_rows(jnp.where(sub < 2, -c2,
                                      jnp.where(sub == 2, q0c + (2.0 * c2) * col.astype(F32), 0.0))))
        for mp in range(2):
            dims = slice(mp * QK_DIM, (mp + 1) * QK_DIM)
            for side in range(2):
                qa_ref[slot, mp, side, :QK_DIM, :] = q_hi[dims]
                qa_ref[slot, mp, side, QK_DIM:HEAD_W, :] = q_lo[dims]
                qa_ref[slot, mp, side, HEAD_W:, :] = biases[side]

    def score_step(mp, i, ci, m):
        r0 = pl.multiple_of(ci * t, t)
        q_aug = qa_ref[i & 1, mp, (ci > i).astype(jnp.int32)]
        corr = corr_ref[pl.ds(pl.multiple_of(jnp.where(ci == i, 0, t), t), t), :]
        s = jnp.dot(ka_ref[mp, pl.ds(r0, t), :], q_aug, preferred_element_type=F32) + corr
        s_refs[mp][pl.ds(r0, t), :] = s
        return jnp.maximum(m, jnp.max(s, axis=0, keepdims=True))

    def value_step(mp, ci, m, acc):
        r0 = pl.multiple_of(ci * t, t)
        e = jnp.exp2(s_refs[mp][pl.ds(r0, t), :] - m).astype(BF16)
        return acc + jnp.dot(vta_ref[:, pl.ds(r0, t)], e, preferred_element_type=F32)

    unroll = min(n, 8)
    m_init = jnp.full((1, t), -1e30, F32)
    acc_init = jnp.zeros((V_DIM + ONES_ROWS, t), F32)

    def scores_and_values(mp_s, i_s, mp_v, m_v):
        return lax.fori_loop(
            0, n, lambda ci, c: (score_step(mp_s, i_s, ci, c[0]), value_step(mp_v, ci, m_v, c[1])),
            (m_init, acc_init), unroll=unroll)

    def normalised(acc):
        return acc[:V_DIM] * (1.0 / acc[V_DIM:V_DIM + 1])

    lam = (jnp.exp(jnp.sum(lq1_ref[...] * lk1_ref[...], axis=-1, keepdims=True))
           - jnp.exp(jnp.sum(lq2_ref[...] * lk2_ref[...], axis=-1, keepdims=True)) + LAMBDA_INIT)

    def write_tile(i, out0, acc1):
        o = (out0 - lam * normalised(acc1)).T
        o_ref[pl.ds(pl.multiple_of(i * t, t), t), :] = (
            _rms(o, g_ref[...]) * (1.0 - LAMBDA_INIT)).astype(o_ref.dtype)

    first, last = jnp.int32(0), jnp.int32(n - 1)
    build_key_side()
    build_query_side(first)
    m0 = lax.fori_loop(0, n, functools.partial(score_step, 0, first), m_init, unroll=unroll)

    def tile_step(i, m0):
        m1, acc0 = scores_and_values(1, i, 0, m0)
        out0_ref[...] = normalised(acc0)
        build_query_side(i + 1)
        m0_next, acc1 = scores_and_values(0, i + 1, 1, m1)
        write_tile(i, out0_ref[...], acc1)
        return m0_next

    m0 = lax.fori_loop(0, n - 1, tile_step, m0)
    m1, acc0 = scores_and_values(1, last, 0, m0)
    out0_ref[...] = normalised(acc0)
    acc1 = lax.fori_loop(0, n, lambda ci, acc: value_step(1, ci, m1, acc), acc_init, unroll=unroll)
    write_tile(last, out0_ref[...], acc1)


def _attn_tile(seq):
    for t in (512, 256, 128):
        if seq % t == 0 and seq * 2 * t * 4 <= V7X_VMEM_BYTES // 4:
            return t
    raise ValueError(f"no attention tile for sequence length {seq}")


def _attention(qk, vt, slopes, lq1, lk1, lq2, lk2, subln_g, *, batch, seq, t):
    vec = pl.BlockSpec((1, QK_DIM), lambda b, h: (0, 0))
    return pl.pallas_call(
        functools.partial(_attn_kernel, seq=seq, t=t),
        grid=(batch, N_HEADS),
        in_specs=[
            pl.BlockSpec(memory_space=pltpu.SMEM),
            vec, vec, vec, vec,
            pl.BlockSpec((seq, HEAD_W), lambda b, h: (b, h)),
            pl.BlockSpec((seq, HEAD_W), lambda b, h: (b, N_HEADS + h)),
            pl.BlockSpec((V_DIM, seq), lambda b, h: (h, b)),
            pl.BlockSpec((1, V_DIM), lambda b, h: (0, 0)),
        ],
        out_specs=pl.BlockSpec((seq, V_DIM), lambda b, h: (b, h)),
        out_shape=jax.ShapeDtypeStruct((batch * seq, D_ATTN), BF16),
        scratch_shapes=[
            pltpu.VMEM((2, seq, 2 * HEAD_W), BF16),
            pltpu.VMEM((V_DIM + ONES_ROWS, seq), BF16),
            pltpu.VMEM((t, t), F32),
            pltpu.VMEM((2 * t, t), F32),
            pltpu.VMEM((2, 2, 2, 2 * HEAD_W, t), BF16),
            pltpu.VMEM((seq, t), F32),
            pltpu.VMEM((seq, t), F32),
            pltpu.VMEM((V_DIM, t), F32),
        ],
        compiler_params=_params(("arbitrary", "arbitrary")),
        name="diff_attn",
    )(slopes, lq1, lk1, lq2, lk2, qk, qk, vt, subln_g)


def _pool_kernel(prev_ref, x_ref, next_ref, w_ref, s_ref, o_ref, *, seq, tm):
    i = pl.program_id(1)
    n = tm + 2 * POOL_HALO
    t = i * tm + lax.broadcasted_iota(jnp.int32, (tm, 1), 0)
    first = i == 0
    last = i == pl.num_programs(1) - 1
    for g, w in enumerate(POOL_WINDOWS):
        cols = slice(g * POOL_GROUP_DIM, (g + 1) * POOL_GROUP_DIM)
        x = x_ref[:, cols]
        prev = jnp.where(first, 0.0, prev_ref[:, cols])
        nxt = jnp.where(last, 0.0, next_ref[:, cols])
        a = jnp.concatenate([prev, x, nxt], axis=0)
        span = 1
        while span < w:
            a = a + pltpu.roll(a, n - span, axis=0)
            span *= 2
        off = POOL_HALO - w // 2
        if off:
            a = pltpu.roll(a, n - off, axis=0)
        wsum = a[:tm]
        lo = jnp.maximum(t - w // 2, 0)
        hi = jnp.minimum(t + w // 2 - 1, seq - 1)
        pooled = wsum / (hi - lo + 1).astype(F32) - x
        y = jnp.dot(pooled.astype(BF16), w_ref[g], preferred_element_type=F32)
        o_ref[:, cols] = (y * s_ref[:, cols]).astype(o_ref.dtype)


def _pool(rest, w_grp, scale, *, batch, seq, col_block, tm=512):
    nt = seq // tm
    hb = tm // POOL_HALO
    n_hblk = seq // POOL_HALO
    return pl.pallas_call(
        functools.partial(_pool_kernel, seq=seq, tm=tm),
        grid=(batch, nt),
        in_specs=[
            pl.BlockSpec((POOL_HALO, D_POOL),
                         lambda b, i: (b * n_hblk + jnp.maximum(i * hb - 1, 0), col_block)),
            pl.BlockSpec((tm, D_POOL), lambda b, i: (b * nt + i, col_block)),
            pl.BlockSpec((POOL_HALO, D_POOL),
                         lambda b, i: (b * n_hblk + jnp.minimum((i + 1) * hb, n_hblk - 1), col_block)),
            pl.BlockSpec((len(POOL_WINDOWS), POOL_GROUP_DIM, POOL_GROUP_DIM), lambda b, i: (0, 0, 0)),
            pl.BlockSpec((1, D_POOL), lambda b, i: (0, 0)),
        ],
        out_specs=pl.BlockSpec((tm, D_POOL), lambda b, i: (b * nt + i, 0)),
        out_shape=jax.ShapeDtypeStruct((batch * seq, D_POOL), BF16),
        compiler_params=_params(("parallel", "arbitrary")),
        name="pool_mixer",
    )(rest, rest, rest, w_grp, scale)


def _mix_kernel(at_ref, y_ref, ga_ref, gp_ref, h_ref, wa_ref, wp_ref, wo_ref, o_ref):
    a = jnp.dot(at_ref[...], wa_ref[...], preferred_element_type=F32)
    p = jnp.dot(y_ref[...], wp_ref[...], preferred_element_type=F32)
    merged = jax.nn.sigmoid(ga_ref[...]) * a + jax.nn.sigmoid(gp_ref[...]) * p
    o_ref[...] = h_ref[...] + jnp.dot(merged.astype(BF16), wo_ref[...], preferred_element_type=F32)


def _mix(attn, y, rest, h, w_a, w_p, w_o, *, tm=256):
    T = h.shape[0]
    const = lambda i: (0, 0)
    return pl.pallas_call(
        _mix_kernel,
        grid=(T // tm,),
        in_specs=[
            pl.BlockSpec((tm, D_ATTN), lambda i: (i, 0)),
            pl.BlockSpec((tm, D_POOL), lambda i: (i, 0)),
            pl.BlockSpec((tm, D_MODEL), lambda i: (i, 0)),
            pl.BlockSpec((tm, D_MODEL), lambda i: (i, 1)),
            pl.BlockSpec((tm, D_MODEL), lambda i: (i, 0)),
            pl.BlockSpec((D_ATTN, D_MODEL), const, pipeline_mode=pl.Buffered(1)),
            pl.BlockSpec((D_POOL, D_MODEL), const, pipeline_mode=pl.Buffered(1)),
            pl.BlockSpec((D_MODEL, D_MODEL), const, pipeline_mode=pl.Buffered(1)),
        ],
        out_specs=pl.BlockSpec((tm, D_MODEL), lambda i: (i, 0)),
        out_shape=jax.ShapeDtypeStruct((T, D_MODEL), F32),
        compiler_params=_params(("parallel",)),
        name="gated_mix",
    )(attn, y, rest, rest, h, w_a, w_p, w_o)


def _trunk(x, p):
    batch, seq, _ = x.shape
    xf = x.reshape(batch * seq, D_MODEL)
    h, u = _ffn(xf, p["ffn1_norm"], p["ffn1_w_gu"], p["ffn1_w_down"], p["mix_norm"], emit_u=True,
                tm=FFN1_ROWS)
    qk = _mm_nn(u, p["w_qk"], BF16, name="proj_qk")
    vt = _mm_nt(p["w_v_t"], u, BF16, name="proj_vt")
    rest = _mm_nn(u, p["w_rest"], F32, name="proj_rest")
    attn = _attention(qk, vt, p["slopes"], p["lq1"], p["lk1"], p["lq2"], p["lk2"], p["subln_g"],
                      batch=batch, seq=seq, t=_attn_tile(seq))
    y = _pool(rest, p["w_pool_grp"], p["pool_scale"], batch=batch, seq=seq,
              col_block=2 * D_MODEL // D_POOL)
    h2 = _mix(attn, y, rest, h, p["w_attn_proj"], p["w_pool_proj"], p["w_out"])
    out = _ffn(h2, p["ffn2_norm"], p["ffn2_w_gu"], p["ffn2_w_down"], p["final_norm"], emit_u=False,
               tm=FFN2_ROWS)
    return out.reshape(batch, seq, D_MODEL)


def kernel(x_prompt, x_sample, ffn1_norm, ffn1_w_gu, ffn1_w_down, mix_norm, w_in, lambda_q1, lambda_k1, lambda_q2, lambda_k2, attn_subln_g, w_attn_proj, w_pool_grp, pool_scale, w_pool_proj, w_out, ffn2_norm, ffn2_w_gu, ffn2_w_down, final_norm):
    l = 0
    w = w_in[l]
    c_v, c_p, c_ga = 2 * D_Q, 2 * D_Q + D_ATTN, 2 * D_Q + D_ATTN + D_POOL
    p = {
        "ffn1_norm": ffn1_norm[l][None], "mix_norm": mix_norm[l][None],
        "ffn2_norm": ffn2_norm[l][None], "final_norm": final_norm[None],
        "ffn1_w_gu": ffn1_w_gu[l].astype(BF16), "ffn1_w_down": ffn1_w_down[l].astype(BF16),
        "ffn2_w_gu": ffn2_w_gu[l].astype(BF16), "ffn2_w_down": ffn2_w_down[l].astype(BF16),
        "w_qk": w[:, :c_v].astype(BF16),
        "w_v_t": w[:, c_v:c_p].T.astype(BF16),
        "w_rest": jnp.concatenate([w[:, c_ga:], w[:, c_p:c_ga]], axis=1).astype(BF16),
        "slopes": jnp.asarray(2.0 ** (-8.0 * np.arange(1, N_HEADS + 1) / N_HEADS), dtype=F32),
        "lq1": lambda_q1[l][None], "lk1": lambda_k1[l][None],
        "lq2": lambda_q2[l][None], "lk2": lambda_k2[l][None],
        "subln_g": attn_subln_g[l][None],
        "w_attn_proj": w_attn_proj[l].astype(BF16),
        "w_pool_grp": w_pool_grp[l].astype(BF16),
        "pool_scale": pool_scale[l][None],
        "w_pool_proj": w_pool_proj[l].astype(BF16),
        "w_out": w_out[l].astype(BF16),
    }
    return (_trunk(x_prompt, p), _trunk(x_sample, p))
```

```python
import functools

import jax
import jax.numpy as jnp
import numpy as np
from jax import lax
from jax.experimental import pallas as pl
from jax.experimental.pallas import tpu as pltpu

D_MODEL = 2048
N_HEADS = 8
QK_DIM = 64
V_DIM = 2 * QK_DIM
HEAD_W = 2 * QK_DIM
D_ATTN = N_HEADS * V_DIM
D_Q = N_HEADS * HEAD_W
ALIBI_SPLIT = 64
BIAS_GROUP = 8
ONES_ROWS = 16
LOG2E = 1.4426950408889634
POOL_WINDOWS = (2, 4, 8, 16)
POOL_GROUP_DIM = 256
D_POOL = len(POOL_WINDOWS) * POOL_GROUP_DIM
POOL_HALO = 8
D_FF = 5632
EPS = 1e-6
LAMBDA_INIT = 0.8 - 0.6 * float(np.exp(-0.3 * 0))

V7X_VMEM_BYTES = 64 * 1024 * 1024
V7X_VMEM_LIMIT_BYTES = V7X_VMEM_BYTES - 8 * 1024 * 1024
V7X_VMEM_LIMIT_MAX_BYTES = V7X_VMEM_BYTES - 512 * 1024
FFN1_ROWS = 512
FFN2_ROWS = 1024

F32 = jnp.float32
BF16 = jnp.bfloat16


def _params(semantics, vmem_limit_bytes=V7X_VMEM_LIMIT_BYTES):
    return pltpu.CompilerParams(dimension_semantics=semantics, vmem_limit_bytes=vmem_limit_bytes)


def _rms(xf, g):
    return xf * lax.rsqrt(jnp.mean(xf * xf, axis=-1, keepdims=True) + EPS) * g


def _ffn_kernel(x_ref, g_ref, wg_ref, wu_ref, wd_ref, g2_ref, *refs, nf, emit_u):
    if emit_u:
        o_ref, u_ref, xn_ref = refs
    else:
        o_ref, xn_ref = refs
    f = pl.program_id(1)

    @pl.when(f == 0)
    def _():
        xn_ref[...] = _rms(x_ref[...], g_ref[...]).astype(BF16)
        o_ref[...] = jnp.zeros_like(o_ref)

    xn = xn_ref[...]
    gate = jnp.dot(xn, wg_ref[...], preferred_element_type=F32)
    up = jnp.dot(xn, wu_ref[...], preferred_element_type=F32)
    act = (gate * jax.nn.sigmoid(gate) * up).astype(BF16)
    o_ref[...] += jnp.dot(act, wd_ref[...], preferred_element_type=F32)

    @pl.when(f == nf - 1)
    def _():
        h = x_ref[...] + 0.5 * o_ref[...]
        if emit_u:
            o_ref[...] = h
            u_ref[...] = _rms(h, g2_ref[...]).astype(BF16)
        else:
            o_ref[...] = _rms(h, g2_ref[...])


def _ffn(x, g, w_gu, w_down, g2, *, emit_u, tm, tf=512):
    T = x.shape[0]
    assert T % tm == 0 and D_FF % tf == 0, (T, tm, tf)
    nf = D_FF // tf
    grid = (T // tm, nf)
    in_specs = [
        pl.BlockSpec((tm, D_MODEL), lambda i, f: (i, 0)),
        pl.BlockSpec((1, D_MODEL), lambda i, f: (0, 0)),
        pl.BlockSpec((D_MODEL, tf), lambda i, f: (0, f)),
        pl.BlockSpec((D_MODEL, tf), lambda i, f: (0, nf + f)),
        pl.BlockSpec((tf, D_MODEL), lambda i, f: (f, 0)),
        pl.BlockSpec((1, D_MODEL), lambda i, f: (0, 0)),
    ]
    row_spec = pl.BlockSpec((tm, D_MODEL), lambda i, f: (i, 0))
    if emit_u:
        out_shape = (jax.ShapeDtypeStruct((T, D_MODEL), F32), jax.ShapeDtypeStruct((T, D_MODEL), BF16))
        out_specs = (row_spec, row_spec)
    else:
        out_shape = jax.ShapeDtypeStruct((T, D_MODEL), F32)
        out_specs = row_spec
    return pl.pallas_call(
        functools.partial(_ffn_kernel, nf=nf, emit_u=emit_u),
        grid=grid, in_specs=in_specs, out_specs=out_specs, out_shape=out_shape,
        scratch_shapes=[pltpu.VMEM((tm, D_MODEL), BF16)],
        compiler_params=_params(("parallel", "arbitrary"),
                                V7X_VMEM_LIMIT_BYTES if tm <= FFN1_ROWS else V7X_VMEM_LIMIT_MAX_BYTES),
        name="ffn_u" if emit_u else "ffn_final",
    )(x, g, w_gu, w_gu, w_down, g2)


def _mm_nn_kernel(x_ref, w_ref, o_ref):
    o_ref[...] = jnp.dot(x_ref[...], w_ref[...], preferred_element_type=F32).astype(o_ref.dtype)


def _mm_nn(x, w, out_dtype, *, tm=2048, tn=1024, name):
    T, K = x.shape
    N = w.shape[1]
    tm = min(tm, T)
    assert T % tm == 0 and N % tn == 0, (T, tm, N, tn)
    return pl.pallas_call(
        _mm_nn_kernel,
        grid=(T // tm, N // tn),
        in_specs=[pl.BlockSpec((tm, K), lambda i, j: (i, 0)),
                  pl.BlockSpec((K, tn), lambda i, j: (0, j))],
        out_specs=pl.BlockSpec((tm, tn), lambda i, j: (i, j)),
        out_shape=jax.ShapeDtypeStruct((T, N), out_dtype),
        compiler_params=_params(("parallel", "arbitrary")),
        name=name,
    )(x, w)


def _mm_nt_kernel(wt_ref, x_ref, o_ref):
    o_ref[...] = lax.dot_general(wt_ref[...], x_ref[...], (((1,), (1,)), ((), ())),
                                 preferred_element_type=F32).astype(o_ref.dtype)


def _mm_nt(wt, x, out_dtype, *, tm=2048, name):
    N, K = wt.shape
    T = x.shape[0]
    tm = min(tm, T)
    assert T % tm == 0, (T, tm)
    return pl.pallas_call(
        _mm_nt_kernel,
        grid=(T // tm,),
        in_specs=[pl.BlockSpec((N, K), lambda i: (0, 0)),
                  pl.BlockSpec((tm, K), lambda i: (i, 0))],
        out_specs=pl.BlockSpec((N, tm), lambda i: (0, i)),
        out_shape=jax.ShapeDtypeStruct((N, T), out_dtype),
        compiler_params=_params(("parallel",)),
        name=name,
    )(wt, x)


def _attn_kernel(slope_ref, lq1_ref, lk1_ref, lq2_ref, lk2_ref, q_ref, k_ref, vt_ref, g_ref, o_ref,
                 ka_ref, vta_ref, relu_ref, corr_ref, qa_ref, s0_ref, s1_ref, out0_ref, *, seq, t):
    h = pl.program_id(1)
    n = seq // t
    c2 = slope_ref[h] * LOG2E
    s_refs = (s0_ref, s1_ref)

    @pl.when((pl.program_id(0) == 0) & (h == 0))
    def _():
        j = lax.broadcasted_iota(jnp.int32, (seq, HEAD_W), 0)
        lane = lax.broadcasted_iota(jnp.int32, (seq, HEAD_W), 1)
        sub = lane & (BIAS_GROUP - 1)
        j_lo = j & (ALIBI_SPLIT - 1)
        pos = jnp.where(lane < 3 * BIAS_GROUP,
                        jnp.where(sub == 0, j_lo, jnp.where(sub == 1, j - j_lo, jnp.where(sub == 2, 1, 0))), 0)
        pos = pos.astype(F32).astype(BF16)
        for mp in range(2):
            ka_ref[mp, :, HEAD_W:] = pos
        vta_ref[V_DIM:, :] = jnp.ones((ONES_ROWS, seq), BF16)
        d = lax.broadcasted_iota(jnp.int32, (t, t), 0) - lax.broadcasted_iota(jnp.int32, (t, t), 1)
        relu_ref[...] = jnp.maximum(d, 0).astype(F32)
        corr_ref[t:, :] = jnp.zeros((t, t), F32)

    def build_key_side():
        kf = k_ref[...].astype(F32)
        for mp in range(2):
            km = kf[:, mp * QK_DIM:(mp + 1) * QK_DIM]
            ka_ref[mp, :, :HEAD_W] = jnp.concatenate([km, km], axis=1).astype(BF16)
        vta_ref[:V_DIM, :] = vt_ref[...]
        corr_ref[:t, :] = (-2.0 * c2) * relu_ref[...]

    def build_query_side(i):
        slot = i & 1
        rows = pl.ds(pl.multiple_of(i * t, t), t)
        qa = q_ref[rows, :].astype(F32).T * (QK_DIM ** -0.5 * LOG2E)
        q_hi = qa.astype(BF16)
        q_lo = (qa - q_hi.astype(F32)).astype(BF16)
        row = lax.broadcasted_iota(jnp.int32, (4 * BIAS_GROUP, t), 0)
        col = lax.broadcasted_iota(jnp.int32, (4 * BIAS_GROUP, t), 1)
        sub = row & (BIAS_GROUP - 1)
        q0c = c2 * (i * t).astype(F32)

        def bias_rows(base):
            hi = base.astype(BF16).astype(F32)
            mid = (base - hi).astype(BF16).astype(F32)
            lo = base - hi - mid
            b = jnp.where(row < BIAS_GROUP, hi, jnp.where(row < 2 * BIAS_GROUP, mid,
                                                          jnp.where(row < 3 * BIAS_GROUP, lo, 0.0)))
            return jnp.concatenate([b, jnp.zeros((HEAD_W - 4 * BIAS_GROUP, t), F32)], axis=0).astype(BF16)

        biases = (bias_rows(jnp.where(sub < 2, c2, jnp.where(sub == 2, -q0c, 0.0))),
                  bias_rows(jnp.where(sub < 2, -c2,
                                      jnp.where(sub == 2, q0c + (2.0 * c2) * col.astype(F32), 0.0))))
        for mp in range(2):
            dims = slice(mp * QK_DIM, (mp + 1) * QK_DIM)
            for side in range(2):
                qa_ref[slot, mp, side, :QK_DIM, :] = q_hi[dims]
                qa_ref[slot, mp, side, QK_DIM:HEAD_W, :] = q_lo[dims]
                qa_ref[slot, mp, side, HEAD_W:, :] = biases[side]

    def score_step(mp, i, ci, m):
        r0 = pl.multiple_of(ci * t, t)
        q_aug = qa_ref[i & 1, mp, (ci > i).astype(jnp.int32)]
        corr = corr_ref[pl.ds(pl.multiple_of(jnp.where(ci == i, 0, t), t), t), :]
        s = jnp.dot(ka_ref[mp, pl.ds(r0, t), :], q_aug, preferred_element_type=F32) + corr
        s_refs[mp][pl.ds(r0, t), :] = s
        return jnp.maximum(m, jnp.max(s, axis=0, keepdims=True))

    def value_step(mp, ci, m, acc):
        r0 = pl.multiple_of(ci * t, t)
        e = jnp.exp2(s_refs[mp][pl.ds(r0, t), :] - m).astype(BF16)
        return acc + jnp.dot(vta_ref[:, pl.ds(r0, t)], e, preferred_element_type=F32)

    unroll = min(n, 8)
    m_init = jnp.full((1, t), -1e30, F32)
    acc_init = jnp.zeros((V_DIM + ONES_ROWS, t), F32)

    def scores_and_values(mp_s, i_s, mp_v, m_v):
        return lax.fori_loop(
            0, n, lambda ci, c: (score_step(mp_s, i_s, ci, c[0]), value_step(mp_v, ci, m_v, c[1])),
            (m_init, acc_init), unroll=unroll)

    def normalised(acc):
        return acc[:V_DIM] * (1.0 / acc[V_DIM:V_DIM + 1])

    lam = (jnp.exp(jnp.sum(lq1_ref[...] * lk1_ref[...], axis=-1, keepdims=True))
           - jnp.exp(jnp.sum(lq2_ref[...] * lk2_ref[...], axis=-1, keepdims=True)) + LAMBDA_INIT)

    def write_tile(i, out0, acc1):
        o = (out0 - lam * normalised(acc1)).T
        o_ref[pl.ds(pl.multiple_of(i * t, t), t), :] = (
            _rms(o, g_ref[...]) * (1.0 - LAMBDA_INIT)).astype(o_ref.dtype)

    first, last = jnp.int32(0), jnp.int32(n - 1)
    build_key_side()
    build_query_side(first)
    m0 = lax.fori_loop(0, n, functools.partial(score_step, 0, first), m_init, unroll=unroll)

    def tile_step(i, m0):
        m1, acc0 = scores_and_values(1, i, 0, m0)
        out0_ref[...] = normalised(acc0)
        build_query_side(i + 1)
        m0_next, acc1 = scores_and_values(0, i + 1, 1, m1)
        write_tile(i, out0_ref[...], acc1)
        return m0_next

    m0 = lax.fori_loop(0, n - 1, tile_step, m0)
    m1, acc0 = scores_and_values(1, last, 0, m0)
    out0_ref[...] = normalised(acc0)
    acc1 = lax.fori_loop(0, n, lambda ci, acc: value_step(1, ci, m1, acc), acc_init, unroll=unroll)
    write_tile(last, out0_ref[...], acc1)


def _attn_tile(seq):
    for t in (512, 256, 128):
        if seq % t == 0 and seq * 2 * t * 4 <= V7X_VMEM_BYTES // 4:
            return t
    raise ValueError(f"no attention tile for sequence length {seq}")


def _attention(qk, vt, slopes, lq1, lk1, lq2, lk2, subln_g, *, batch, seq, t):
    vec = pl.BlockSpec((1, QK_DIM), lambda b, h: (0, 0))
    return pl.pallas_call(
        functools.partial(_attn_kernel, seq=seq, t=t),
        grid=(batch, N_HEADS),
        in_specs=[
            pl.BlockSpec(memory_space=pltpu.SMEM),
            vec, vec, vec, vec,
            pl.BlockSpec((seq, HEAD_W), lambda b, h: (b, h)),
            pl.BlockSpec((seq, HEAD_W), lambda b, h: (b, N_HEADS + h)),
            pl.BlockSpec((V_DIM, seq), lambda b, h: (h, b)),
            pl.BlockSpec((1, V_DIM), lambda b, h: (0, 0)),
        ],
        out_specs=pl.BlockSpec((seq, V_DIM), lambda b, h: (b, h)),
        out_shape=jax.ShapeDtypeStruct((batch * seq, D_ATTN), BF16),
        scratch_shapes=[
            pltpu.VMEM((2, seq, 2 * HEAD_W), BF16),
            pltpu.VMEM((V_DIM + ONES_ROWS, seq), BF16),
            pltpu.VMEM((t, t), F32),
            pltpu.VMEM((2 * t, t), F32),
            pltpu.VMEM((2, 2, 2, 2 * HEAD_W, t), BF16),
            pltpu.VMEM((seq, t), F32),
            pltpu.VMEM((seq, t), F32),
            pltpu.VMEM((V_DIM, t), F32),
        ],
        compiler_params=_params(("arbitrary", "arbitrary")),
        name="diff_attn",
    )(slopes, lq1, lk1, lq2, lk2, qk, qk, vt, subln_g)


def _pool_kernel(prev_ref, x_ref, next_ref, w_ref, s_ref, o_ref, *, seq, tm):
    i = pl.program_id(1)
    n = tm + 2 * POOL_HALO
    t = i * tm + lax.broadcasted_iota(jnp.int32, (tm, 1), 0)
    first = i == 0
    last = i == pl.num_programs(1) - 1
    for g, w in enumerate(POOL_WINDOWS):
        cols = slice(g * POOL_GROUP_DIM, (g + 1) * POOL_GROUP_DIM)
        x = x_ref[:, cols]
        prev = jnp.where(first, 0.0, prev_ref[:, cols])
        nxt = jnp.where(last, 0.0, next_ref[:, cols])
        a = jnp.concatenate([prev, x, nxt], axis=0)
        span = 1
        while span < w:
            a = a + pltpu.roll(a, n - span, axis=0)
            span *= 2
        off = POOL_HALO - w // 2
        if off:
            a = pltpu.roll(a, n - off, axis=0)
        wsum = a[:tm]
        lo = jnp.maximum(t - w // 2, 0)
        hi = jnp.minimum(t + w // 2 - 1, seq - 1)
        pooled = wsum / (hi - lo + 1).astype(F32) - x
        y = jnp.dot(pooled.astype(BF16), w_ref[g], preferred_element_type=F32)
        o_ref[:, cols] = (y * s_ref[:, cols]).astype(o_ref.dtype)


def _pool(rest, w_grp, scale, *, batch, seq, col_block, tm=1024):
    tm = min(tm, seq)
    assert seq % tm == 0, (seq, tm)
    nt = seq // tm
    hb = tm // POOL_HALO
    n_hblk = seq // POOL_HALO
    return pl.pallas_call(
        functools.partial(_pool_kernel, seq=seq, tm=tm),
        grid=(batch, nt),
        in_specs=[
            pl.BlockSpec((POOL_HALO, D_POOL),
                         lambda b, i: (b * n_hblk + jnp.maximum(i * hb - 1, 0), col_block)),
            pl.BlockSpec((tm, D_POOL), lambda b, i: (b * nt + i, col_block)),
            pl.BlockSpec((POOL_HALO, D_POOL),
                         lambda b, i: (b * n_hblk + jnp.minimum((i + 1) * hb, n_hblk - 1), col_block)),
            pl.BlockSpec((len(POOL_WINDOWS), POOL_GROUP_DIM, POOL_GROUP_DIM), lambda b, i: (0, 0, 0)),
            pl.BlockSpec((1, D_POOL), lambda b, i: (0, 0)),
        ],
        out_specs=pl.BlockSpec((tm, D_POOL), lambda b, i: (b * nt + i, 0)),
        out_shape=jax.ShapeDtypeStruct((batch * seq, D_POOL), BF16),
        compiler_params=_params(("parallel", "arbitrary")),
        name="pool_mixer",
    )(rest, rest, rest, w_grp, scale)


def _mix_kernel(at_ref, y_ref, ga_ref, gp_ref, h_ref, wa_ref, wp_ref, wo_ref, o_ref):
    a = jnp.dot(at_ref[...], wa_ref[...], preferred_element_type=F32)
    p = jnp.dot(y_ref[...], wp_ref[...], preferred_element_type=F32)
    merged = jax.nn.sigmoid(ga_ref[...]) * a + jax.nn.sigmoid(gp_ref[...]) * p
    o_ref[...] = h_ref[...] + jnp.dot(merged.astype(BF16), wo_ref[...], preferred_element_type=F32)


def _mix(attn, y, rest, h, w_a, w_p, w_o, *, tm=256):
    T = h.shape[0]
    assert T % tm == 0, (T, tm)
    const = lambda i: (0, 0)
    return pl.pallas_call(
        _mix_kernel,
        grid=(T // tm,),
        in_specs=[
            pl.BlockSpec((tm, D_ATTN), lambda i: (i, 0)),
            pl.BlockSpec((tm, D_POOL), lambda i: (i, 0)),
            pl.BlockSpec((tm, D_MODEL), lambda i: (i, 0)),
            pl.BlockSpec((tm, D_MODEL), lambda i: (i, 1)),
            pl.BlockSpec((tm, D_MODEL), lambda i: (i, 0)),
            pl.BlockSpec((D_ATTN, D_MODEL), const, pipeline_mode=pl.Buffered(1)),
            pl.BlockSpec((D_POOL, D_MODEL), const, pipeline_mode=pl.Buffered(1)),
            pl.BlockSpec((D_MODEL, D_MODEL), const, pipeline_mode=pl.Buffered(1)),
        ],
        out_specs=pl.BlockSpec((tm, D_MODEL), lambda i: (i, 0)),
        out_shape=jax.ShapeDtypeStruct((T, D_MODEL), F32),
        compiler_params=_params(("parallel",)),
        name="gated_mix",
    )(attn, y, rest, rest, h, w_a, w_p, w_o)


def _trunk(x, p):
    batch, seq, _ = x.shape
    xf = x.reshape(batch * seq, D_MODEL)
    h, u = _ffn(xf, p["ffn1_norm"], p["ffn1_w_gu"], p["ffn1_w_down"], p["mix_norm"], emit_u=True,
                tm=FFN1_ROWS)
    qk = _mm_nn(u, p["w_qk"], BF16, name="proj_qk")
    vt = _mm_nt(p["w_v_t"], u, BF16, name="proj_vt")
    rest = _mm_nn(u, p["w_rest"], F32, name="proj_rest")
    attn = _attention(qk, vt, p["slopes"], p["lq1"], p["lk1"], p["lq2"], p["lk2"], p["subln_g"],
                      batch=batch, seq=seq, t=_attn_tile(seq))
    y = _pool(rest, p["w_pool_grp"], p["pool_scale"], batch=batch, seq=seq,
              col_block=2 * D_MODEL // D_POOL)
    h2 = _mix(attn, y, rest, h, p["w_attn_proj"], p["w_pool_proj"], p["w_out"])
    out = _ffn(h2, p["ffn2_norm"], p["ffn2_w_gu"], p["ffn2_w_down"], p["final_norm"], emit_u=False,
               tm=FFN2_ROWS)
    return out.reshape(batch, seq, D_MODEL)


def kernel(x_prompt, x_sample, ffn1_norm, ffn1_w_gu, ffn1_w_down, mix_norm, w_in, lambda_q1, lambda_k1, lambda_q2, lambda_k2, attn_subln_g, w_attn_proj, w_pool_grp, pool_scale, w_pool_proj, w_out, ffn2_norm, ffn2_w_gu, ffn2_w_down, final_norm):
    l = 0
    w = w_in[l]
    c_v, c_p, c_ga = 2 * D_Q, 2 * D_Q + D_ATTN, 2 * D_Q + D_ATTN + D_POOL
    p = {
        "ffn1_norm": ffn1_norm[l][None], "mix_norm": mix_norm[l][None],
        "ffn2_norm": ffn2_norm[l][None], "final_norm": final_norm[None],
        "ffn1_w_gu": ffn1_w_gu[l].astype(BF16), "ffn1_w_down": ffn1_w_down[l].astype(BF16),
        "ffn2_w_gu": ffn2_w_gu[l].astype(BF16), "ffn2_w_down": ffn2_w_down[l].astype(BF16),
        "w_qk": w[:, :c_v].astype(BF16),
        "w_v_t": w[:, c_v:c_p].T.astype(BF16),
        "w_rest": jnp.concatenate([w[:, c_ga:], w[:, c_p:c_ga]], axis=1).astype(BF16),
        "slopes": jnp.asarray(2.0 ** (-8.0 * np.arange(1, N_HEADS + 1) / N_HEADS), dtype=F32),
        "lq1": lambda_q1[l][None], "lk1": lambda_k1[l][None],
        "lq2": lambda_q2[l][None], "lk2": lambda_k2[l][None],
        "subln_g": attn_subln_g[l][None],
        "w_attn_proj": w_attn_proj[l].astype(BF16),
        "w_pool_grp": w_pool_grp[l].astype(BF16),
        "pool_scale": pool_scale[l][None],
        "w_pool_proj": w_pool_proj[l].astype(BF16),
        "w_out": w_out[l].astype(BF16),
    }
    return (_trunk(x_prompt, p), _trunk(x_sample, p))
```

```python
import functools

import jax
import jax.numpy as jnp
import numpy as np
from jax import lax
from jax.experimental import pallas as pl
from jax.experimental.pallas import tpu as pltpu

D_MODEL = 2048
N_HEADS = 8
QK_DIM = 64
V_DIM = 2 * QK_DIM
HEAD_W = 2 * QK_DIM
D_ATTN = N_HEADS * V_DIM
D_Q = N_HEADS * HEAD_W
ALIBI_SPLIT = 64
BIAS_GROUP = 8
ONES_ROWS = 16
LOG2E = 1.4426950408889634
POOL_WINDOWS = (2, 4, 8, 16)
POOL_GROUP_DIM = 256
D_POOL = len(POOL_WINDOWS) * POOL_GROUP_DIM
POOL_HALO = 8
D_FF = 5632
EPS = 1e-6
LAMBDA_INIT = 0.8 - 0.6 * float(np.exp(-0.3 * 0))

V7X_VMEM_BYTES = 64 * 1024 * 1024
V7X_VMEM_LIMIT_BYTES = V7X_VMEM_BYTES - 8 * 1024 * 1024
V7X_VMEM_LIMIT_MAX_BYTES = V7X_VMEM_BYTES - 512 * 1024
FFN1_ROWS = 512
FFN2_ROWS = 1024

F32 = jnp.float32
BF16 = jnp.bfloat16


def _params(semantics, vmem_limit_bytes=V7X_VMEM_LIMIT_BYTES):
    return pltpu.CompilerParams(dimension_semantics=semantics, vmem_limit_bytes=vmem_limit_bytes)


def _rms(xf, g):
    return xf * lax.rsqrt(jnp.mean(xf * xf, axis=-1, keepdims=True) + EPS) * g


def _ffn_kernel(x_ref, g_ref, wg_ref, wu_ref, wd_ref, g2_ref, *refs, nf, emit_u):
    if emit_u:
        o_ref, u_ref, xn_ref = refs
    else:
        o_ref, xn_ref = refs
    _ffn_step(pl.program_id(1), x_ref, g_ref, wg_ref, wu_ref, wd_ref, g2_ref, o_ref,
              u_ref if emit_u else None, xn_ref, nf=nf)


def _ffn_step(f, x_ref, g_ref, wg_ref, wu_ref, wd_ref, g2_ref, o_ref, u_ref, xn_ref, *, nf):
    emit_u = u_ref is not None

    @pl.when(f == 0)
    def _():
        xn_ref[...] = _rms(x_ref[...], g_ref[...]).astype(BF16)
        o_ref[...] = jnp.zeros_like(o_ref)

    xn = xn_ref[...]
    gate = jnp.dot(xn, wg_ref[...], preferred_element_type=F32)
    up = jnp.dot(xn, wu_ref[...], preferred_element_type=F32)
    act = (gate * jax.nn.sigmoid(gate) * up).astype(BF16)
    o_ref[...] += jnp.dot(act, wd_ref[...], preferred_element_type=F32)

    @pl.when(f == nf - 1)
    def _():
        h = x_ref[...] + 0.5 * o_ref[...]
        if emit_u:
            o_ref[...] = h
            u_ref[...] = _rms(h, g2_ref[...]).astype(BF16)
        else:
            o_ref[...] = _rms(h, g2_ref[...])


def _ffn(x, g, w_gu, w_down, g2, *, emit_u, tm, tf=512):
    T = x.shape[0]
    assert T % tm == 0 and D_FF % tf == 0, (T, tm, tf)
    nf = D_FF // tf
    grid = (T // tm, nf)
    deep = tm <= FFN1_ROWS
    weight_mode = dict(pipeline_mode=pl.Buffered(3)) if deep else {}
    in_specs = [
        pl.BlockSpec((tm, D_MODEL), lambda i, f: (i, 0)),
        pl.BlockSpec((1, D_MODEL), lambda i, f: (0, 0)),
        pl.BlockSpec((D_MODEL, tf), lambda i, f: (0, f), **weight_mode),
        pl.BlockSpec((D_MODEL, tf), lambda i, f: (0, nf + f), **weight_mode),
        pl.BlockSpec((tf, D_MODEL), lambda i, f: (f, 0), **weight_mode),
        pl.BlockSpec((1, D_MODEL), lambda i, f: (0, 0)),
    ]
    row_spec = pl.BlockSpec((tm, D_MODEL), lambda i, f: (i, 0))
    if emit_u:
        out_shape = (jax.ShapeDtypeStruct((T, D_MODEL), F32), jax.ShapeDtypeStruct((T, D_MODEL), BF16))
        out_specs = (row_spec, row_spec)
    else:
        out_shape = jax.ShapeDtypeStruct((T, D_MODEL), F32)
        out_specs = row_spec
    if deep:
        assert emit_u

        def outer(x_hbm, g_hbm, wg_hbm, wu_hbm, wd_hbm, g2_hbm, o_hbm, u_hbm, xn_ref, count_ref):
            count_ref[0] = 0

            def step(x_ref, g_ref, wg_ref, wu_ref, wd_ref, g2_ref, o_ref, u_ref):
                c = count_ref[0]
                count_ref[0] = c + 1
                _ffn_step(lax.rem(c, nf), x_ref, g_ref, wg_ref, wu_ref, wd_ref, g2_ref, o_ref, u_ref, xn_ref,
                          nf=nf)

            pltpu.emit_pipeline(step, grid=grid, in_specs=in_specs, out_specs=list(out_specs))(
                x_hbm, g_hbm, wg_hbm, wu_hbm, wd_hbm, g2_hbm, o_hbm, u_hbm)

        any_spec = pl.BlockSpec(memory_space=pl.ANY)
        return pl.pallas_call(
            outer, in_specs=[any_spec] * 6, out_specs=(any_spec, any_spec), out_shape=out_shape,
            scratch_shapes=[pltpu.VMEM((tm, D_MODEL), BF16), pltpu.SMEM((1,), jnp.int32)],
            compiler_params=pltpu.CompilerParams(vmem_limit_bytes=V7X_VMEM_LIMIT_BYTES),
            name="ffn_u",
        )(x, g, w_gu, w_gu, w_down, g2)
    return pl.pallas_call(
        functools.partial(_ffn_kernel, nf=nf, emit_u=emit_u),
        grid=grid, in_specs=in_specs, out_specs=out_specs, out_shape=out_shape,
        scratch_shapes=[pltpu.VMEM((tm, D_MODEL), BF16)],
        compiler_params=_params(("parallel", "arbitrary"),
                                V7X_VMEM_LIMIT_BYTES if tm <= FFN1_ROWS else V7X_VMEM_LIMIT_MAX_BYTES),
        name="ffn_u" if emit_u else "ffn_final",
    )(x, g, w_gu, w_gu, w_down, g2)


def _mm_nn_kernel(x_ref, w_ref, o_ref):
    o_ref[...] = jnp.dot(x_ref[...], w_ref[...], preferred_element_type=F32).astype(o_ref.dtype)


def _mm_nn(x, w, out_dtype, *, tm=2048, tn=1024, name):
    T, K = x.shape
    N = w.shape[1]
    tm = min(tm, T)
    assert T % tm == 0 and N % tn == 0, (T, tm, N, tn)
    return pl.pallas_call(
        _mm_nn_kernel,
        grid=(T // tm, N // tn),
        in_specs=[pl.BlockSpec((tm, K), lambda i, j: (i, 0)),
                  pl.BlockSpec((K, tn), lambda i, j: (0, j))],
        out_specs=pl.BlockSpec((tm, tn), lambda i, j: (i, j)),
        out_shape=jax.ShapeDtypeStruct((T, N), out_dtype),
        compiler_params=_params(("parallel", "arbitrary")),
        name=name,
    )(x, w)


def _mm_nt_kernel(wt_ref, x_ref, o_ref):
    o_ref[...] = lax.dot_general(wt_ref[...], x_ref[...], (((1,), (1,)), ((), ())),
                                 preferred_element_type=F32).astype(o_ref.dtype)


def _mm_nt(wt, x, out_dtype, *, tm=2048, name):
    N, K = wt.shape
    T = x.shape[0]
    tm = min(tm, T)
    assert T % tm == 0, (T, tm)
    return pl.pallas_call(
        _mm_nt_kernel,
        grid=(T // tm,),
        in_specs=[pl.BlockSpec((N, K), lambda i: (0, 0)),
                  pl.BlockSpec((tm, K), lambda i: (i, 0))],
        out_specs=pl.BlockSpec((N, tm), lambda i: (0, i)),
        out_shape=jax.ShapeDtypeStruct((N, T), out_dtype),
        compiler_params=_params(("parallel",)),
        name=name,
    )(wt, x)


def _attn_kernel(slope_ref, lq1_ref, lk1_ref, lq2_ref, lk2_ref, q_ref, k_ref, vt_ref, g_ref, o_ref,
                 ka_ref, vta_ref, relu_ref, corr_ref, qa_ref, s0_ref, s1_ref, out0_ref, *, seq, t):
    h = pl.program_id(1)
    n = seq // t
    c2 = slope_ref[h] * LOG2E
    s_refs = (s0_ref, s1_ref)

    @pl.when((pl.program_id(0) == 0) & (h == 0))
    def _():
        j = lax.broadcasted_iota(jnp.int32, (seq, HEAD_W), 0)
        lane = lax.broadcasted_iota(jnp.int32, (seq, HEAD_W), 1)
        sub = lane & (BIAS_GROUP - 1)
        j_lo = j & (ALIBI_SPLIT - 1)
        pos = jnp.where(lane < 3 * BIAS_GROUP,
                        jnp.where(sub == 0, j_lo, jnp.where(sub == 1, j - j_lo, jnp.where(sub == 2, 1, 0))), 0)
        pos = pos.astype(F32).astype(BF16)
        for mp in range(2):
            ka_ref[mp, :, HEAD_W:] = pos
        vta_ref[V_DIM:, :] = jnp.ones((ONES_ROWS, seq), BF16)
        d = lax.broadcasted_iota(jnp.int32, (t, t), 0) - lax.broadcasted_iota(jnp.int32, (t, t), 1)
        relu_ref[...] = jnp.maximum(d, 0).astype(F32)
        corr_ref[t:, :] = jnp.zeros((t, t), F32)

    def build_key_side():
        kf = k_ref[...].astype(F32)
        for mp in range(2):
            km = kf[:, mp * QK_DIM:(mp + 1) * QK_DIM]
            ka_ref[mp, :, :HEAD_W] = jnp.concatenate([km, km], axis=1).astype(BF16)
        vta_ref[:V_DIM, :] = vt_ref[...]
        corr_ref[:t, :] = (-2.0 * c2) * relu_ref[...]

    def build_query_side(i):
        slot = i & 1
        rows = pl.ds(pl.multiple_of(i * t, t), t)
        qa = q_ref[rows, :].astype(F32).T * (QK_DIM ** -0.5 * LOG2E)
        q_hi = qa.astype(BF16)
        q_lo = (qa - q_hi.astype(F32)).astype(BF16)
        row = lax.broadcasted_iota(jnp.int32, (4 * BIAS_GROUP, t), 0)
        col = lax.broadcasted_iota(jnp.int32, (4 * BIAS_GROUP, t), 1)
        sub = row & (BIAS_GROUP - 1)
        q0c = c2 * (i * t).astype(F32)

        def bias_rows(base):
            hi = base.astype(BF16).astype(F32)
            mid = (base - hi).astype(BF16).astype(F32)
            lo = base - hi - mid
            b = jnp.where(row < BIAS_GROUP, hi, jnp.where(row < 2 * BIAS_GROUP, mid,
                                                          jnp.where(row < 3 * BIAS_GROUP, lo, 0.0)))
            return jnp.concatenate([b, jnp.zeros((HEAD_W - 4 * BIAS_GROUP, t), F32)], axis=0).astype(BF16)

        biases = (bias_rows(jnp.where(sub < 2, c2, jnp.where(sub == 2, -q0c, 0.0))),
                  bias_rows(jnp.where(sub < 2, -c2,
                                      jnp.where(sub == 2, q0c + (2.0 * c2) * col.astype(F32), 0.0))))
        for mp in range(2):
            dims = slice(mp * QK_DIM, (mp + 1) * QK_DIM)
            for side in range(2):
                qa_ref[slot, mp, side, :QK_DIM, :] = q_hi[dims]
                qa_ref[slot, mp, side, QK_DIM:HEAD_W, :] = q_lo[dims]
                qa_ref[slot, mp, side, HEAD_W:, :] = biases[side]

    def score_step(mp, i, ci, m):
        r0 = pl.multiple_of(ci * t, t)
        q_aug = qa_ref[i & 1, mp, (ci > i).astype(jnp.int32)]
        corr = corr_ref[pl.ds(pl.multiple_of(jnp.where(ci == i, 0, t), t), t), :]
        s = jnp.dot(ka_ref[mp, pl.ds(r0, t), :], q_aug, preferred_element_type=F32) + corr
        s_refs[mp][pl.ds(r0, t), :] = s
        return jnp.maximum(m, jnp.max(s, axis=0, keepdims=True))

    def value_step(mp, ci, m, acc):
        r0 = pl.multiple_of(ci * t, t)
        e = jnp.exp2(s_refs[mp][pl.ds(r0, t), :] - m).astype(BF16)
        return acc + jnp.dot(vta_ref[:, pl.ds(r0, t)], e, preferred_element_type=F32)

    unroll = min(n, 8)
    m_init = jnp.full((1, t), -1e30, F32)
    acc_init = jnp.zeros((V_DIM + ONES_ROWS, t), F32)

    def scores_and_values(mp_s, i_s, mp_v, m_v):
        return lax.fori_loop(
            0, n, lambda ci, c: (score_step(mp_s, i_s, ci, c[0]), value_step(mp_v, ci, m_v, c[1])),
            (m_init, acc_init), unroll=unroll)

    def normalised(acc):
        return acc[:V_DIM] * (1.0 / acc[V_DIM:V_DIM + 1])

    lam = (jnp.exp(jnp.sum(lq1_ref[...] * lk1_ref[...], axis=-1, keepdims=True))
           - jnp.exp(jnp.sum(lq2_ref[...] * lk2_ref[...], axis=-1, keepdims=True)) + LAMBDA_INIT)

    def write_tile(i, out0, acc1):
        o = (out0 - lam * normalised(acc1)).T
        o_ref[pl.ds(pl.multiple_of(i * t, t), t), :] = (
            _rms(o, g_ref[...]) * (1.0 - LAMBDA_INIT)).astype(o_ref.dtype)

    first, last = jnp.int32(0), jnp.int32(n - 1)
    build_key_side()
    build_query_side(first)
    m0 = lax.fori_loop(0, n, functools.partial(score_step, 0, first), m_init, unroll=unroll)

    def tile_step(i, m0):
        m1, acc0 = scores_and_values(1, i, 0, m0)
        out0_ref[...] = normalised(acc0)
        build_query_side(i + 1)
        m0_next, acc1 = scores_and_values(0, i + 1, 1, m1)
        write_tile(i, out0_ref[...], acc1)
        return m0_next

    m0 = lax.fori_loop(0, n - 1, tile_step, m0)
    m1, acc0 = scores_and_values(1, last, 0, m0)
    out0_ref[...] = normalised(acc0)
    acc1 = lax.fori_loop(0, n, lambda ci, acc: value_step(1, ci, m1, acc), acc_init, unroll=unroll)
    write_tile(last, out0_ref[...], acc1)


def _attn_tile(seq):
    for t in (512, 256, 128):
        if seq % t == 0 and seq * 2 * t * 4 <= V7X_VMEM_BYTES // 4:
            return t
    raise ValueError(f"no attention tile for sequence length {seq}")


def _attention(qk, vt, slopes, lq1, lk1, lq2, lk2, subln_g, *, batch, seq, t):
    vec = pl.BlockSpec((1, QK_DIM), lambda b, h: (0, 0))
    return pl.pallas_call(
        functools.partial(_attn_kernel, seq=seq, t=t),
        grid=(batch, N_HEADS),
        in_specs=[
            pl.BlockSpec(memory_space=pltpu.SMEM),
            vec, vec, vec, vec,
            pl.BlockSpec((seq, HEAD_W), lambda b, h: (b, h)),
            pl.BlockSpec((seq, HEAD_W), lambda b, h: (b, N_HEADS + h)),
            pl.BlockSpec((V_DIM, seq), lambda b, h: (h, b)),
            pl.BlockSpec((1, V_DIM), lambda b, h: (0, 0)),
        ],
        out_specs=pl.BlockSpec((seq, V_DIM), lambda b, h: (b, h)),
        out_shape=jax.ShapeDtypeStruct((batch * seq, D_ATTN), BF16),
        scratch_shapes=[
            pltpu.VMEM((2, seq, 2 * HEAD_W), BF16),
            pltpu.VMEM((V_DIM + ONES_ROWS, seq), BF16),
            pltpu.VMEM((t, t), F32),
            pltpu.VMEM((2 * t, t), F32),
            pltpu.VMEM((2, 2, 2, 2 * HEAD_W, t), BF16),
            pltpu.VMEM((seq, t), F32),
            pltpu.VMEM((seq, t), F32),
            pltpu.VMEM((V_DIM, t), F32),
        ],
        compiler_params=_params(("arbitrary", "arbitrary")),
        name="diff_attn",
    )(slopes, lq1, lk1, lq2, lk2, qk, qk, vt, subln_g)


def _pool_kernel(prev_ref, x_ref, next_ref, w_ref, s_ref, o_ref, *, seq, tm):
    i = pl.program_id(1)
    n = tm + 2 * POOL_HALO
    t = i * tm + lax.broadcasted_iota(jnp.int32, (tm, 1), 0)
    first = i == 0
    last = i == pl.num_programs(1) - 1
    for g, w in enumerate(POOL_WINDOWS):
        cols = slice(g * POOL_GROUP_DIM, (g + 1) * POOL_GROUP_DIM)
        x = x_ref[:, cols]
        prev = jnp.where(first, 0.0, prev_ref[:, cols])
        nxt = jnp.where(last, 0.0, next_ref[:, cols])
        a = jnp.concatenate([prev, x, nxt], axis=0)
        span = 1
        while span < w:
            a = a + pltpu.roll(a, n - span, axis=0)
            span *= 2
        off = POOL_HALO - w // 2
        if off:
            a = pltpu.roll(a, n - off, axis=0)
        wsum = a[:tm]
        lo = jnp.maximum(t - w // 2, 0)
        hi = jnp.minimum(t + w // 2 - 1, seq - 1)
        pooled = wsum / (hi - lo + 1).astype(F32) - x
        y = jnp.dot(pooled.astype(BF16), w_ref[g], preferred_element_type=F32)
        o_ref[:, cols] = (y * s_ref[:, cols]).astype(o_ref.dtype)


def _pool(rest, w_grp, scale, *, batch, seq, col_block, tm=1024):
    tm = min(tm, seq)
    assert seq % tm == 0, (seq, tm)
    nt = seq // tm
    hb = tm // POOL_HALO
    n_hblk = seq // POOL_HALO
    return pl.pallas_call(
        functools.partial(_pool_kernel, seq=seq, tm=tm),
        grid=(batch, nt),
        in_specs=[
            pl.BlockSpec((POOL_HALO, D_POOL),
                         lambda b, i: (b * n_hblk + jnp.maximum(i * hb - 1, 0), col_block)),
            pl.BlockSpec((tm, D_POOL), lambda b, i: (b * nt + i, col_block)),
            pl.BlockSpec((POOL_HALO, D_POOL),
                         lambda b, i: (b * n_hblk + jnp.minimum((i + 1) * hb, n_hblk - 1), col_block)),
            pl.BlockSpec((len(POOL_WINDOWS), POOL_GROUP_DIM, POOL_GROUP_DIM), lambda b, i: (0, 0, 0)),
            pl.BlockSpec((1, D_POOL), lambda b, i: (0, 0)),
        ],
        out_specs=pl.BlockSpec((tm, D_POOL), lambda b, i: (b * nt + i, 0)),
        out_shape=jax.ShapeDtypeStruct((batch * seq, D_POOL), BF16),
        compiler_params=_params(("parallel", "arbitrary")),
        name="pool_mixer",
    )(rest, rest, rest, w_grp, scale)


def _mix_kernel(at_ref, y_ref, ga_ref, gp_ref, h_ref, wa_ref, wp_ref, wo_ref, o_ref):
    a = jnp.dot(at_ref[...], wa_ref[...], preferred_element_type=F32)
    p = jnp.dot(y_ref[...], wp_ref[...], preferred_element_type=F32)
    merged = jax.nn.sigmoid(ga_ref[...]) * a + jax.nn.sigmoid(gp_ref[...]) * p
    o_ref[...] = h_ref[...] + jnp.dot(merged.astype(BF16), wo_ref[...], preferred_element_type=F32)


def _mix(attn, y, rest, h, w_a, w_p, w_o, *, tm=256):
    T = h.shape[0]
    assert T % tm == 0, (T, tm)
    const = lambda i: (0, 0)
    return pl.pallas_call(
        _mix_kernel,
        grid=(T // tm,),
        in_specs=[
            pl.BlockSpec((tm, D_ATTN), lambda i: (i, 0)),
            pl.BlockSpec((tm, D_POOL), lambda i: (i, 0)),
            pl.BlockSpec((tm, D_MODEL), lambda i: (i, 0)),
            pl.BlockSpec((tm, D_MODEL), lambda i: (i, 1)),
            pl.BlockSpec((tm, D_MODEL), lambda i: (i, 0)),
            pl.BlockSpec((D_ATTN, D_MODEL), const, pipeline_mode=pl.Buffered(1)),
            pl.BlockSpec((D_POOL, D_MODEL), const, pipeline_mode=pl.Buffered(1)),
            pl.BlockSpec((D_MODEL, D_MODEL), const, pipeline_mode=pl.Buffered(1)),
        ],
        out_specs=pl.BlockSpec((tm, D_MODEL), lambda i: (i, 0)),
        out_shape=jax.ShapeDtypeStruct((T, D_MODEL), F32),
        compiler_params=_params(("parallel",)),
        name="gated_mix",
    )(attn, y, rest, rest, h, w_a, w_p, w_o)


def _trunk(x, p):
    batch, seq, _ = x.shape
    xf = x.reshape(batch * seq, D_MODEL)
    h, u = _ffn(xf, p["ffn1_norm"], p["ffn1_w_gu"], p["ffn1_w_down"], p["mix_norm"], emit_u=True,
                tm=FFN1_ROWS)
    qk = _mm_nn(u, p["w_qk"], BF16, name="proj_qk")
    vt = _mm_nt(p["w_v_t"], u, BF16, name="proj_vt")
    rest = _mm_nn(u, p["w_rest"], F32, name="proj_rest")
    attn = _attention(qk, vt, p["slopes"], p["lq1"], p["lk1"], p["lq2"], p["lk2"], p["subln_g"],
                      batch=batch, seq=seq, t=_attn_tile(seq))
    y = _pool(rest, p["w_pool_grp"], p["pool_scale"], batch=batch, seq=seq,
              col_block=2 * D_MODEL // D_POOL)
    h2 = _mix(attn, y, rest, h, p["w_attn_proj"], p["w_pool_proj"], p["w_out"])
    out = _ffn(h2, p["ffn2_norm"], p["ffn2_w_gu"], p["ffn2_w_down"], p["final_norm"], emit_u=False,
               tm=FFN2_ROWS)
    return out.reshape(batch, seq, D_MODEL)


def kernel(x_prompt, x_sample, ffn1_norm, ffn1_w_gu, ffn1_w_down, mix_norm, w_in, lambda_q1, lambda_k1, lambda_q2, lambda_k2, attn_subln_g, w_attn_proj, w_pool_grp, pool_scale, w_pool_proj, w_out, ffn2_norm, ffn2_w_gu, ffn2_w_down, final_norm):
    l = 0
    w = w_in[l]
    c_v, c_p, c_ga = 2 * D_Q, 2 * D_Q + D_ATTN, 2 * D_Q + D_ATTN + D_POOL
    p = {
        "ffn1_norm": ffn1_norm[l][None], "mix_norm": mix_norm[l][None],
        "ffn2_norm": ffn2_norm[l][None], "final_norm": final_norm[None],
        "ffn1_w_gu": ffn1_w_gu[l].astype(BF16), "ffn1_w_down": ffn1_w_down[l].astype(BF16),
        "ffn2_w_gu": ffn2_w_gu[l].astype(BF16), "ffn2_w_down": ffn2_w_down[l].astype(BF16),
        "w_qk": w[:, :c_v].astype(BF16),
        "w_v_t": w[:, c_v:c_p].T.astype(BF16),
        "w_rest": jnp.concatenate([w[:, c_ga:], w[:, c_p:c_ga]], axis=1).astype(BF16),
        "slopes": jnp.asarray(2.0 ** (-8.0 * np.arange(1, N_HEADS + 1) / N_HEADS), dtype=F32),
        "lq1": lambda_q1[l][None], "lk1": lambda_k1[l][None],
        "lq2": lambda_q2[l][None], "lk2": lambda_k2[l][None],
        "subln_g": attn_subln_g[l][None],
        "w_attn_proj": w_attn_proj[l].astype(BF16),
        "w_pool_grp": w_pool_grp[l].astype(BF16),
        "pool_scale": pool_scale[l][None],
        "w_pool_proj": w_pool_proj[l].astype(BF16),
        "w_out": w_out[l].astype(BF16),
    }
    return (_trunk(x_prompt, p), _trunk(x_sample, p))
```
